```python
import math
import jax
import jax.numpy as jnp
from jax import lax
import numpy as np

D_MODEL = 1024
BATCH = 16
SEQ = 2048
DEPTH = 4
DEC_BATCH = 8
DEC_SEQ = 64
PAST_LEN = 1024

CHUNK = 64
N_A_LAYERS = DEPTH // 2
N_B_LAYERS = DEPTH - N_A_LAYERS
SSM_WIDTH = D_MODEL
SSM_GROUP_CH = 16
SSM_GROUPS = SSM_WIDTH // SSM_GROUP_CH
SSM_STATE = 64
DT_MIN = 1e-3
DT_MAX = 1e-1
HEAD_DIM = 64
N_HEADS = D_MODEL // HEAD_DIM
N_KV_HEADS = 4
Q_PER_KV = N_HEADS // N_KV_HEADS
ATTN_WIDTH = N_HEADS * HEAD_DIM
KV_WIDTH = N_KV_HEADS * HEAD_DIM
WINDOW = 128
WINDOW_CHUNKS = WINDOW // CHUNK
ROPE_THETA = 10000.0
NEG_INF = -1e30
DEEPNORM_ALPHA = (2.0 * DEPTH) ** 0.25
DEEPNORM_BETA = (8.0 * DEPTH) ** -0.25
LN_EPS = 1e-5

kernel_name = 'yoco_s5_swa_sink_streaming_step'


def layer_norm(x, g, b):
    xf = x.astype(jnp.float32)
    mu = jnp.mean(xf, axis=-1, keepdims=True)
    var = jnp.mean(jnp.square(xf - mu), axis=-1, keepdims=True)
    y = (xf - mu) * lax.rsqrt(var + LN_EPS) * g.astype(jnp.float32) + b.astype(jnp.float32)
    return y.astype(x.dtype)


def ada_params(c, w, b):
    cond = jax.nn.silu(c) @ w + b
    shift, scale, gate = jnp.split(cond, 3, axis=-1)
    return shift[:, None], scale[:, None], gate[:, None]


def rope(x, pos):
    half = HEAD_DIM // 2
    inv_freq = jnp.power(ROPE_THETA, -jnp.arange(half, dtype=jnp.float32) / half)
    ang = pos.astype(jnp.float32)[:, None] * inv_freq[None, :]
    cos = jnp.cos(ang)[None, :, None, :]
    sin = jnp.sin(ang)[None, :, None, :]
    xf = x.astype(jnp.float32)
    x1, x2 = xf[..., :half], xf[..., half:]
    out = jnp.concatenate([x1 * cos - x2 * sin, x2 * cos + x1 * sin], axis=-1)
    return out.astype(x.dtype)


def s5_scan(u, h0, a_re, a_im, b_re, b_im, c_re, c_im, d, log_dt):
    f32 = jnp.float32
    bsz, seq_len, _ = u.shape
    lam = lax.complex(a_re.astype(f32), a_im.astype(f32))
    dt = jnp.exp(log_dt.astype(f32))[:, None]
    a_bar = jnp.exp(lam * dt)
    b_bar = ((a_bar - 1.0) / lam)[..., None] * lax.complex(b_re.astype(f32), b_im.astype(f32))
    uf = u.astype(f32)
    ug = uf.reshape(bsz, seq_len, SSM_GROUPS, SSM_GROUP_CH).astype(jnp.complex64)
    bu = jnp.einsum('gpc,blgc->blgp', b_bar, ug)
    bu = bu.at[:, 0].add(a_bar[None] * h0)
    a_seq = jnp.broadcast_to(a_bar, (1, seq_len) + a_bar.shape)

    def combine(e1, e2):
        a1, b1 = e1
        a2, b2 = e2
        return a1 * a2, a2 * b1 + b2

    _, h = lax.associative_scan(combine, (a_seq, bu), axis=1)
    c_mat = lax.complex(c_re.astype(f32), c_im.astype(f32))
    y = jnp.real(jnp.einsum('gcp,blgp->blgc', c_mat, h)).reshape(bsz, seq_len, SSM_WIDTH)
    y = y + d.astype(f32) * uf
    return y.astype(u.dtype), h[:, -1]


def ssm_branch(hmod, la, h0, p):
    uz = hmod @ p['w_in_a'][la]
    u, z = jnp.split(uz, 2, axis=-1)
    y, h_last = s5_scan(u, h0, p['ssm_a_re'][la], p['ssm_a_im'][la], p['ssm_b_re'][la], p['ssm_b_im'][la],
                        p['ssm_c_re'][la], p['ssm_c_im'][la], p['ssm_d'][la], p['ssm_log_dt'][la])
    g = jax.nn.gelu(y)
    y = g * jax.nn.sigmoid(g @ p['w_glu'][la] + p['b_glu'][la])
    y = y * jax.nn.silu(z)
    return y @ p['w_out_a'][la], h_last


def shared_kv(x, w_kv, pos):
    bsz, seq_len, _ = x.shape
    kv = x @ w_kv
    k, v = jnp.split(kv, 2, axis=-1)
    k = rope(k.reshape(bsz, seq_len, N_KV_HEADS, HEAD_DIM), pos)
    v = v.reshape(bsz, seq_len, N_KV_HEADS, HEAD_DIM)
    return k, v


def sink_attention(q, k, v, sinks, mask):
    s = jnp.einsum('ncqkgd,ncjkd->nckgqj', q, k).astype(jnp.float32) * (HEAD_DIM ** -0.5)
    if mask is not None:
        s = jnp.where(mask, s, NEG_INF)
    sink = sinks.astype(jnp.float32).reshape(1, 1, N_KV_HEADS, Q_PER_KV, 1, 1)
    m = jnp.maximum(jnp.max(s, axis=-1, keepdims=True), sink)
    e = jnp.exp(s - m)
    denom = jnp.sum(e, axis=-1, keepdims=True) + jnp.exp(sink - m)
    probs = (e / denom).astype(v.dtype)
    return jnp.einsum('nckgqj,ncjkd->ncqkgd', probs, v)


def banded_window_attention(q, k, v, sinks):
    bsz, seq_len = q.shape[:2]
    n_chunks = seq_len // CHUNK
    qb = q.reshape(bsz, n_chunks, CHUNK, N_KV_HEADS, Q_PER_KV, HEAD_DIM)
    pad = ((0, 0), (WINDOW, 0), (0, 0), (0, 0))
    kp = jnp.pad(k, pad).reshape(bsz, n_chunks + WINDOW_CHUNKS, CHUNK, N_KV_HEADS, HEAD_DIM)
    vp = jnp.pad(v, pad).reshape(bsz, n_chunks + WINDOW_CHUNKS, CHUNK, N_KV_HEADS, HEAD_DIM)
    kb = jnp.concatenate([kp[:, i:i + n_chunks] for i in range(WINDOW_CHUNKS + 1)], axis=2)
    vb = jnp.concatenate([vp[:, i:i + n_chunks] for i in range(WINDOW_CHUNKS + 1)], axis=2)
    key_pos = (jnp.arange(n_chunks)[:, None] * CHUNK
               + jnp.arange((WINDOW_CHUNKS + 1) * CHUNK)[None, :] - WINDOW)
    mask = (key_pos >= 0)[None, :, None, None, None, :]
    o = sink_attention(qb, kb, vb, sinks, mask)
    return o.reshape(bsz, seq_len, ATTN_WIDTH)


def cached_window_attention(q, k_new, v_new, cache_k, cache_v, sinks):
    bsz, t_new = q.shape[:2]
    qb = q.reshape(bsz, 1, t_new, N_KV_HEADS, Q_PER_KV, HEAD_DIM)
    kb = jnp.concatenate([cache_k, k_new], axis=1)[:, None]
    vb = jnp.concatenate([cache_v, v_new], axis=1)[:, None]
    o = sink_attention(qb, kb, vb, sinks, None)
    return o.reshape(bsz, t_new, ATTN_WIDTH)


def attn_branch(hmod, lb, pos, k_sh, v_sh, cache_k, cache_v, p):
    bsz, seq_len, _ = hmod.shape
    qz = hmod @ p['w_in_b'][lb]
    q, z = jnp.split(qz, 2, axis=-1)
    q = rope(q.reshape(bsz, seq_len, N_HEADS, HEAD_DIM), pos)
    if cache_k is None:
        o = banded_window_attention(q, k_sh, v_sh, p['attn_sinks'][lb])
    else:
        o = cached_window_attention(q, k_sh, v_sh, cache_k, cache_v, p['attn_sinks'][lb])
    o = o * jax.nn.silu(z)
    return o @ p['w_out_b'][lb]


def run_trunk(x, c, pos, h0, cache_k, cache_v, p):
    new_h = []
    k_sh = None
    v_sh = None
    for layer in range(DEPTH):
        shift, scale, gate = ada_params(c, p['w_ada'][layer], p['b_ada'][layer])
        if layer == N_A_LAYERS:
            k_sh, v_sh = shared_kv(x, p['w_kv'], pos)
        hmod = x * (1.0 + scale) + shift
        if layer < N_A_LAYERS:
            out, h_last = ssm_branch(hmod, layer, h0[layer], p)
            new_h.append(h_last)
        else:
            out = attn_branch(hmod, layer - N_A_LAYERS, pos, k_sh, v_sh, cache_k, cache_v, p)
        x = layer_norm(DEEPNORM_ALPHA * x + gate * out, p['ln_g'][layer], p['ln_b'][layer])
    h_stack = jnp.stack(new_h)
    state = jnp.stack([jnp.real(h_stack), jnp.imag(h_stack)], axis=-1)
    return x, state, k_sh, v_sh


def setup_inputs(seed: int = 0) -> dict:
    key = jax.random.key(seed)
    ks = jax.random.split(key, 32)
    f32 = jnp.float32

    def nrm(k, shape, s):
        return jax.random.normal(k, shape, f32) * s

    kv_rows = min(WINDOW, PAST_LEN)
    n_idx = jnp.arange(SSM_STATE, dtype=f32)
    ga = (N_A_LAYERS, SSM_GROUPS, SSM_STATE)
    return {
        'x_prompt': nrm(ks[0], (BATCH, SEQ, D_MODEL), 1.0),
        'x_sample': nrm(ks[1], (DEC_BATCH, DEC_SEQ, D_MODEL), 1.0),
        'state_ssm': nrm(ks[2], (N_A_LAYERS, DEC_BATCH, SSM_GROUPS, SSM_STATE, 2), 0.1),
        'cache_k': nrm(ks[3], (DEC_BATCH, kv_rows, N_KV_HEADS, HEAD_DIM), 1.0),
        'cache_v': nrm(ks[4], (DEC_BATCH, kv_rows, N_KV_HEADS, HEAD_DIM), 1.0),
        'c_prompt': nrm(ks[5], (BATCH, D_MODEL), 1.0),
        'c_sample': nrm(ks[6], (DEC_BATCH, D_MODEL), 1.0),
        'w_ada': nrm(ks[7], (DEPTH, D_MODEL, 3 * D_MODEL), 0.5 * D_MODEL ** -0.5),
        'b_ada': nrm(ks[8], (DEPTH, 3 * D_MODEL), 0.01),
        'ln_g': 1.0 + nrm(ks[9], (DEPTH, D_MODEL), 0.02),
        'ln_b': nrm(ks[10], (DEPTH, D_MODEL), 0.02),
        'w_in_a': nrm(ks[11], (N_A_LAYERS, D_MODEL, 2 * SSM_WIDTH), D_MODEL ** -0.5),
        'ssm_a_re': -0.5 + nrm(ks[12], ga, 0.01),
        'ssm_a_im': math.pi * n_idx + nrm(ks[13], ga, 0.01),
        'ssm_b_re': nrm(ks[14], ga + (SSM_GROUP_CH,), (2.0 * SSM_GROUP_CH) ** -0.5),
        'ssm_b_im': nrm(ks[15], ga + (SSM_GROUP_CH,), (2.0 * SSM_GROUP_CH) ** -0.5),
        'ssm_c_re': nrm(ks[16], (N_A_LAYERS, SSM_GROUPS, SSM_GROUP_CH, SSM_STATE), SSM_STATE ** -0.5),
        'ssm_c_im': nrm(ks[17], (N_A_LAYERS, SSM_GROUPS, SSM_GROUP_CH, SSM_STATE), SSM_STATE ** -0.5),
        'ssm_d': nrm(ks[18], (N_A_LAYERS, SSM_WIDTH), 1.0),
        'ssm_log_dt': jax.random.uniform(ks[19], (N_A_LAYERS, SSM_GROUPS), f32,
                                         math.log(DT_MIN), math.log(DT_MAX)),
        'w_glu': nrm(ks[20], (N_A_LAYERS, SSM_WIDTH, SSM_WIDTH), SSM_WIDTH ** -0.5),
        'b_glu': nrm(ks[21], (N_A_LAYERS, SSM_WIDTH), 0.01),
        'w_out_a': nrm(ks[22], (N_A_LAYERS, SSM_WIDTH, D_MODEL), SSM_WIDTH ** -0.5 * DEEPNORM_BETA),
        'w_kv': nrm(ks[23], (D_MODEL, 2 * KV_WIDTH), D_MODEL ** -0.5),
        'w_in_b': nrm(ks[24], (N_B_LAYERS, D_MODEL, 2 * ATTN_WIDTH), D_MODEL ** -0.5),
        'attn_sinks': nrm(ks[25], (N_B_LAYERS, N_HEADS), 0.5),
        'w_out_b': nrm(ks[26], (N_B_LAYERS, ATTN_WIDTH, D_MODEL), ATTN_WIDTH ** -0.5 * DEEPNORM_BETA),
    }


def reference(x_prompt, x_sample, state_ssm, cache_k, cache_v, c_prompt, c_sample,
              w_ada, b_ada, ln_g, ln_b, w_in_a, ssm_a_re, ssm_a_im, ssm_b_re, ssm_b_im,
              ssm_c_re, ssm_c_im, ssm_d, ssm_log_dt, w_glu, b_glu, w_out_a,
              w_kv, w_in_b, attn_sinks, w_out_b):
    p = dict(w_ada=w_ada, b_ada=b_ada, ln_g=ln_g, ln_b=ln_b, w_in_a=w_in_a,
             ssm_a_re=ssm_a_re, ssm_a_im=ssm_a_im, ssm_b_re=ssm_b_re, ssm_b_im=ssm_b_im,
             ssm_c_re=ssm_c_re, ssm_c_im=ssm_c_im, ssm_d=ssm_d, ssm_log_dt=ssm_log_dt,
             w_glu=w_glu, b_glu=b_glu, w_out_a=w_out_a, w_kv=w_kv, w_in_b=w_in_b,
             attn_sinks=attn_sinks, w_out_b=w_out_b)
    pos_prompt = jnp.arange(x_prompt.shape[1], dtype=jnp.int32)
    pos_sample = PAST_LEN + jnp.arange(x_sample.shape[1], dtype=jnp.int32)
    h0_prompt = jnp.zeros((N_A_LAYERS, x_prompt.shape[0], SSM_GROUPS, SSM_STATE), jnp.complex64)
    h0_sample = lax.complex(state_ssm[..., 0].astype(jnp.float32), state_ssm[..., 1].astype(jnp.float32))
    y_prompt, ssm_p, k_p, v_p = run_trunk(x_prompt, c_prompt, pos_prompt, h0_prompt, None, None, p)
    y_sample, ssm_s, k_s, v_s = run_trunk(x_sample, c_sample, pos_sample, h0_sample, cache_k, cache_v, p)
    rows = min(WINDOW, x_prompt.shape[1])
    return (y_prompt, y_sample, ssm_p, k_p[:, -rows:], v_p[:, -rows:], ssm_s, k_s, v_s)
```

```python
import functools
import math

import jax
import jax.numpy as jnp
from jax import lax
from jax.experimental import pallas as pl
from jax.experimental.pallas import tpu as pltpu

D_MODEL = 1024
DEPTH = 4
CHUNK = 64
N_A_LAYERS = 2
N_B_LAYERS = 2
SSM_GROUP_CH = 16
SSM_GROUPS = 64
SSM_STATE = 64
HEAD_DIM = 64
N_HEADS = 16
N_KV_HEADS = 4
Q_PER_KV = 4
KV_WIDTH = N_KV_HEADS * HEAD_DIM
WINDOW = 128
WINDOW_CHUNKS = WINDOW // CHUNK
PAST_LEN = 1024
ROPE_THETA = 10000.0
NEG_INF = -1e30
DEEPNORM_ALPHA = (2.0 * DEPTH) ** 0.25
LN_EPS = 1e-5

LANES = 128
STRIP_GROUPS = LANES // SSM_GROUP_CH
N_STRIPS = SSM_GROUPS // STRIP_GROUPS
STRIP_HALF = STRIP_GROUPS * SSM_STATE
STRIP_W = 2 * STRIP_HALF
SCAN_W = 256
S5_ROWS = 512
KEYS = (WINDOW_CHUNKS + 1) * CHUNK
ATTN_ROWS = 256
KV_ROWS = 512
HEAD_TILE = Q_PER_KV * HEAD_DIM
VMEM_LIMIT = 56 * 1024 * 1024

_BF = jnp.bfloat16
_F32 = jnp.float32


def _dot(a, b):
    return jnp.dot(a, b, preferred_element_type=_F32)


def _const_spec(shape):
    nd = len(shape)
    return pl.BlockSpec(shape, lambda *_: (0,) * nd, pipeline_mode=pl.Buffered(1))


def _layer_norm(r, g, b):
    mu = jnp.mean(r, axis=-1, keepdims=True)
    d = r - mu
    var = jnp.mean(d * d, axis=-1, keepdims=True)
    return d * lax.rsqrt(var + LN_EPS) * g + b


def _ada_kernel(c_ref, w_ref, b_ref, o_ref):
    c = c_ref[...]
    o_ref[0] = _dot(jax.nn.silu(c).astype(_BF), w_ref[0]) + b_ref[0]


def _ada_params(c_all, w_ada, b_ada):
    n = c_all.shape[0]
    return pl.pallas_call(
        _ada_kernel,
        grid=(DEPTH, 3),
        in_specs=[pl.BlockSpec((n, D_MODEL), lambda l, j: (0, 0)),
                  pl.BlockSpec((1, D_MODEL, D_MODEL), lambda l, j: (l, 0, j)),
                  pl.BlockSpec((1, 1, D_MODEL), lambda l, j: (l, 0, j))],
        out_specs=pl.BlockSpec((1, n, D_MODEL), lambda l, j: (l, 0, j)),
        out_shape=jax.ShapeDtypeStruct((DEPTH, n, 3 * D_MODEL), _F32),
        name="ada_params",
    )(c_all, w_ada.astype(_BF), b_ada.reshape(DEPTH, 1, 3 * D_MODEL))


def _s5_prep_kernel(are_ref, aim_ref, ldt_ref, bre_ref, bim_ref, oar_ref, oai_ref, obr_ref, obi_ref):
    a_re, a_im = are_ref[0], aim_ref[0]
    dt = jnp.exp(ldt_ref[0])
    mag = jnp.exp(a_re * dt)
    ab_re = mag * jnp.cos(a_im * dt)
    ab_im = mag * jnp.sin(a_im * dt)
    nr, ni = ab_re - 1.0, ab_im
    den = a_re * a_re + a_im * a_im
    f_re = (nr * a_re + ni * a_im) / den
    f_im = (ni * a_re - nr * a_im) / den
    b_re, b_im = bre_ref[0], bim_ref[0]
    oar_ref[0] = ab_re
    oai_ref[0] = ab_im
    obr_ref[0] = f_re * b_re - f_im * b_im
    obi_ref[0] = f_re * b_im + f_im * b_re


def _s5_prep(a_re, a_im, log_dt, b_re, b_im):
    w = SSM_STATE * SSM_GROUP_CH
    rep = lambda a: jnp.repeat(a, SSM_GROUP_CH, axis=-1)
    ldt = jnp.broadcast_to(log_dt[:, :, None], (N_A_LAYERS, SSM_GROUPS, w))
    spec = pl.BlockSpec((1, SSM_GROUPS, w), lambda l: (l, 0, 0))
    shp = jax.ShapeDtypeStruct((N_A_LAYERS, SSM_GROUPS, w), _F32)
    ab_re, ab_im, bb_re, bb_im = pl.pallas_call(
        _s5_prep_kernel, grid=(N_A_LAYERS,), in_specs=[spec] * 5, out_specs=[spec] * 4,
        out_shape=[shp] * 4, name="s5_prep",
    )(rep(a_re), rep(a_im), ldt, b_re.reshape(N_A_LAYERS, SSM_GROUPS, w), b_im.reshape(N_A_LAYERS, SSM_GROUPS, w))
    gp = (N_A_LAYERS, SSM_GROUPS, SSM_STATE, SSM_GROUP_CH)
    return (ab_re.reshape(gp)[..., 0], ab_im.reshape(gp)[..., 0], bb_re.reshape(gp), bb_im.reshape(gp))


def _s5_matrices(ab_re, ab_im, bb_re, bb_im, c_re, c_im):
    eye = jnp.eye(STRIP_GROUPS, dtype=_F32)
    nl = N_A_LAYERS
    bb = jnp.stack([bb_re, bb_im], axis=1).reshape(nl, 2, N_STRIPS, STRIP_GROUPS, SSM_STATE, SSM_GROUP_CH)
    bs = jnp.einsum('lrjgpc,gh->ljgcrhp', bb, eye).reshape(nl, N_STRIPS, LANES, STRIP_W)
    cc = jnp.stack([c_re, -c_im], axis=1).reshape(nl, 2, N_STRIPS, STRIP_GROUPS, SSM_GROUP_CH, SSM_STATE)
    cs = jnp.einsum('lrjgcp,gh->ljrhpgc', cc, eye).reshape(nl, N_STRIPS, STRIP_W, LANES)
    ar = ab_re.reshape(nl, N_STRIPS, 1, STRIP_HALF)
    ai = ab_im.reshape(nl, N_STRIPS, 1, STRIP_HALF)
    return bs.astype(_BF), cs.astype(_BF), ar, ai


def _state_to_strips(state):
    nb = state.shape[0]
    s = state.reshape(nb, N_STRIPS, STRIP_GROUPS, SSM_STATE, 2)
    return s.transpose(1, 0, 4, 2, 3).reshape(N_STRIPS, nb, STRIP_W)


def _strips_to_state(h):
    nb = h.shape[1]
    s = h.reshape(N_STRIPS, nb, 2, STRIP_GROUPS, SSM_STATE)
    return s.transpose(1, 0, 3, 4, 2).reshape(nb, SSM_GROUPS, SSM_STATE, 2)


def _s5_layer_kernel(x_ref, shift_ref, scale_ref, gate_ref, h0_ref, win_ref, bs_ref, cs_ref, ar_ref, ai_ref,
                     d_ref, wglu_ref, bglu_ref, wout_ref, lng_ref, lnb_ref,
                     y_ref, hout_ref,
                     h_scr, u_scr, z_scr, bu_scr, yy_scr, *, nb, steps):
    rows = nb * steps
    i = pl.program_id(0)

    @pl.when(i == 0)
    def _():
        h_scr[...] = h0_ref[...]

    x3 = x_ref[...].reshape(steps, nb, D_MODEL)
    hmod = (x3 * (1.0 + scale_ref[...])[None] + shift_ref[...][None]).reshape(rows, D_MODEL)
    uz = _dot(hmod.astype(_BF), win_ref[...])
    for j in range(N_STRIPS):
        u_scr[j] = uz[:, j * LANES:(j + 1) * LANES]
    z_scr[...] = uz[:, D_MODEL:]

    def strip(j, carry):
        u_j = u_scr[j]
        bu_scr[...] = _dot(u_j.astype(_BF), bs_ref[j])
        a_re_row, a_im_row = ar_ref[j], ai_ref[j]
        for s in range(STRIP_HALF // SCAN_W):
            lo_re, lo_im = s * SCAN_W, STRIP_HALF + s * SCAN_W
            a_re = jnp.broadcast_to(a_re_row[:, lo_re:lo_re + SCAN_W], (nb, SCAN_W))
            a_im = jnp.broadcast_to(a_im_row[:, lo_re:lo_re + SCAN_W], (nb, SCAN_W))

            def step(t, h):
                h_re, h_im = h
                r0 = pl.multiple_of(t * nb, nb)
                bu_re = bu_scr[pl.ds(r0, nb), lo_re:lo_re + SCAN_W]
                bu_im = bu_scr[pl.ds(r0, nb), lo_im:lo_im + SCAN_W]
                n_re = a_re * h_re - a_im * h_im + bu_re
                n_im = a_re * h_im + a_im * h_re + bu_im
                bu_scr[pl.ds(r0, nb), lo_re:lo_re + SCAN_W] = n_re
                bu_scr[pl.ds(r0, nb), lo_im:lo_im + SCAN_W] = n_im
                return n_re, n_im

            h_re, h_im = lax.fori_loop(
                0, steps, step,
                (h_scr[j, :, lo_re:lo_re + SCAN_W], h_scr[j, :, lo_im:lo_im + SCAN_W]), unroll=4)
            h_scr[j, :, lo_re:lo_re + SCAN_W] = h_re
            h_scr[j, :, lo_im:lo_im + SCAN_W] = h_im
        yy_scr[j] = _dot(bu_scr[...].astype(_BF), cs_ref[j]) + d_ref[j] * u_j
        return carry

    lax.fori_loop(0, N_STRIPS, strip, 0)

    y = jnp.concatenate([yy_scr[j] for j in range(N_STRIPS)], axis=1)
    g = jax.nn.gelu(y)
    y = g * jax.nn.sigmoid(_dot(g.astype(_BF), wglu_ref[...]) + bglu_ref[...])
    y = y * jax.nn.silu(z_scr[...])
    out = _dot(y.astype(_BF), wout_ref[...]).reshape(steps, nb, D_MODEL)
    r = (DEEPNORM_ALPHA * x3 + gate_ref[...][None] * out).reshape(rows, D_MODEL)
    y_ref[...] = _layer_norm(r, lng_ref[...], lnb_ref[...])

    @pl.when(i == pl.num_programs(0) - 1)
    def _():
        hout_ref[...] = h_scr[...]


def _s5_layer(x_tb, nb, ada, h0, win, bs, cs, ar, ai, d, wglu, bglu, wout, lng, lnb):
    total = x_tb.shape[0]
    rows = min(S5_ROWS, total)
    steps = rows // nb
    shift, scale, gate = ada
    row_spec = pl.BlockSpec((rows, D_MODEL), lambda i: (i, 0))
    vec = lambda a: a.reshape(1, D_MODEL)
    kern = functools.partial(_s5_layer_kernel, nb=nb, steps=steps)
    return pl.pallas_call(
        kern,
        grid=(total // rows,),
        in_specs=[row_spec,
                  _const_spec((nb, D_MODEL)), _const_spec((nb, D_MODEL)), _const_spec((nb, D_MODEL)),
                  _const_spec((N_STRIPS, nb, STRIP_W)),
                  _const_spec((D_MODEL, 2 * D_MODEL)),
                  _const_spec((N_STRIPS, LANES, STRIP_W)),
                  _const_spec((N_STRIPS, STRIP_W, LANES)),
                  _const_spec((N_STRIPS, 1, STRIP_HALF)), _const_spec((N_STRIPS, 1, STRIP_HALF)),
                  _const_spec((N_STRIPS, 1, LANES)),
                  _const_spec((D_MODEL, D_MODEL)), _const_spec((1, D_MODEL)),
                  _const_spec((D_MODEL, D_MODEL)),
                  _const_spec((1, D_MODEL)), _const_spec((1, D_MODEL))],
        out_specs=[row_spec, pl.BlockSpec((N_STRIPS, nb, STRIP_W), lambda i: (0, 0, 0))],
        out_shape=[jax.ShapeDtypeStruct((total, D_MODEL), _F32),
                   jax.ShapeDtypeStruct((N_STRIPS, nb, STRIP_W), _F32)],
        scratch_shapes=[pltpu.VMEM((N_STRIPS, nb, STRIP_W), _F32),
                        pltpu.VMEM((N_STRIPS, rows, LANES), _F32),
                        pltpu.VMEM((rows, D_MODEL), _F32),
                        pltpu.VMEM((rows, STRIP_W), _F32),
                        pltpu.VMEM((N_STRIPS, rows, LANES), _F32)],
        compiler_params=pltpu.CompilerParams(dimension_semantics=("arbitrary",), vmem_limit_bytes=VMEM_LIMIT),
        name="s5_layer",
    )(x_tb, shift, scale, gate, h0, win, bs, cs, ar, ai, d.reshape(N_STRIPS, 1, LANES),
      wglu, vec(bglu), wout, vec(lng), vec(lnb))


def _rope_tables(pos):
    half = HEAD_DIM // 2
    inv_freq = jnp.power(ROPE_THETA, -jnp.arange(half, dtype=_F32) / half)
    ang = pos.astype(_F32)[:, None] * inv_freq[None, :]
    cos, sin = jnp.cos(ang), jnp.sin(ang)
    zero = jnp.zeros_like(sin)
    cos_t = jnp.tile(cos, (1, LANES // half))
    sin_lo = jnp.tile(jnp.concatenate([-sin, zero], axis=1), (1, LANES // HEAD_DIM))
    sin_hi = jnp.tile(jnp.concatenate([zero, sin], axis=1), (1, LANES // HEAD_DIM))
    return cos_t, sin_lo, sin_hi


def _rope_tile(x, cos_t, sin_lo, sin_hi):
    half = HEAD_DIM // 2
    return x * cos_t + pltpu.roll(x, LANES - half, 1) * sin_lo + pltpu.roll(x, half, 1) * sin_hi


def _kv_kernel(x_ref, w_ref, cos_ref, slo_ref, shi_ref, k_ref, v_ref, krep_ref, vrep_ref):
    kv = _dot(x_ref[0].astype(_BF), w_ref[...])
    cos_t, sin_lo, sin_hi = cos_ref[...], slo_ref[...], shi_ref[...]
    first_head = lax.broadcasted_iota(jnp.int32, (1, LANES), 1) < HEAD_DIM
    for src, (full_ref, rep_ref) in enumerate(((k_ref, krep_ref), (v_ref, vrep_ref))):
        for t in range(KV_WIDTH // LANES):
            tile = kv[:, src * KV_WIDTH + t * LANES: src * KV_WIDTH + (t + 1) * LANES]
            if src == 0:
                tile = _rope_tile(tile, cos_t, sin_lo, sin_hi)
            full_ref[0, :, t * LANES:(t + 1) * LANES] = tile
            swapped = pltpu.roll(tile, HEAD_DIM, 1)
            even = jnp.where(first_head, tile, swapped).astype(_BF)
            odd = jnp.where(first_head, swapped, tile).astype(_BF)
            for h, val in ((2 * t, even), (2 * t + 1, odd)):
                for c in range(HEAD_TILE // LANES):
                    lo = h * HEAD_TILE + c * LANES
                    rep_ref[0, :, lo:lo + LANES] = val


def _shared_kv(x, w_kv, tables):
    bsz, seq, _ = x.shape
    rows = min(KV_ROWS, seq)
    tab_spec = pl.BlockSpec((rows, LANES), lambda b, i: (i, 0))
    return pl.pallas_call(
        _kv_kernel,
        grid=(bsz, seq // rows),
        in_specs=[pl.BlockSpec((1, rows, D_MODEL), lambda b, i: (b, i, 0)),
                  _const_spec((D_MODEL, 2 * KV_WIDTH)), tab_spec, tab_spec, tab_spec],
        out_specs=[pl.BlockSpec((1, rows, KV_WIDTH), lambda b, i: (b, i, 0))] * 2
        + [pl.BlockSpec((1, rows, N_KV_HEADS * HEAD_TILE), lambda b, i: (b, i, 0))] * 2,
        out_shape=[jax.ShapeDtypeStruct((bsz, seq, KV_WIDTH), _F32)] * 2
        + [jax.ShapeDtypeStruct((bsz, seq, N_KV_HEADS * HEAD_TILE), _BF)] * 2,
        compiler_params=pltpu.CompilerParams(dimension_semantics=("arbitrary", "arbitrary"),
                                             vmem_limit_bytes=VMEM_LIMIT),
        name="shared_kv",
    )(x, w_kv, *tables)


def _attn_layer_kernel(sinks_ref, x_ref, shift_ref, scale_ref, gate_ref, win_ref, wout_ref, cos_ref, slo_ref,
                       shi_ref, krep_ref, vrep_ref, lng_ref, lnb_ref, y_ref, q_scr, o_scr, *, rows, banded):
    x = x_ref[0]
    hmod = x * (1.0 + scale_ref[0]) + shift_ref[0]
    qz = _dot(hmod.astype(_BF), win_ref[...])
    cos_t, sin_lo, sin_hi = cos_ref[...], slo_ref[...], shi_ref[...]
    for t in range(D_MODEL // LANES):
        q_t = _rope_tile(qz[:, t * LANES:(t + 1) * LANES], cos_t, sin_lo, sin_hi)
        q_scr[:, t * LANES:(t + 1) * LANES] = q_t * (HEAD_DIM ** -0.5)

    lane_head = lax.broadcasted_iota(jnp.int32, (1, HEAD_TILE), 1) // HEAD_DIM
    key_idx = lax.broadcasted_iota(jnp.int32, (1, KEYS), 1)
    chunks = rows // CHUNK

    def chunk_body(c, carry):
        r0 = pl.multiple_of(c * CHUNK, CHUNK)
        if banded:
            n = pl.program_id(1) * chunks + c
            start = pl.multiple_of(jnp.maximum(n - WINDOW_CHUNKS, 0) * CHUNK, CHUNK)
            valid = jnp.minimum(n + 1, WINDOW_CHUNKS + 1) * CHUNK
        else:
            start, valid = 0, KEYS
        for kv in range(N_KV_HEADS):
            lanes = slice(kv * HEAD_TILE, (kv + 1) * HEAD_TILE)
            q_kv = q_scr[pl.ds(r0, CHUNK), lanes]
            q_stack = jnp.concatenate(
                [jnp.where(lane_head == g, q_kv, 0.0) for g in range(Q_PER_KV)], axis=0).astype(_BF)
            k_win = krep_ref[0, pl.ds(start, KEYS), lanes]
            v_win = vrep_ref[0, pl.ds(start, KEYS), lanes]
            s = lax.dot_general(q_stack, k_win, (((1,), (1,)), ((), ())), preferred_element_type=_F32)
            s = jnp.where(key_idx < valid, s, NEG_INF)
            sink = jnp.concatenate(
                [jnp.full((CHUNK, 1), sinks_ref[kv * Q_PER_KV + g], _F32) for g in range(Q_PER_KV)], axis=0)
            m = jnp.maximum(jnp.max(s, axis=-1, keepdims=True), sink)
            e = jnp.exp(s - m)
            denom = jnp.sum(e, axis=-1, keepdims=True) + jnp.exp(sink - m)
            probs = (e * (1.0 / denom)).astype(_BF)
            o_all = _dot(probs, v_win)
            o_kv = o_all[0:CHUNK]
            for g in range(1, Q_PER_KV):
                o_kv = jnp.where(lane_head == g, o_all[g * CHUNK:(g + 1) * CHUNK], o_kv)
            o_scr[pl.ds(r0, CHUNK), lanes] = o_kv
        return carry

    lax.fori_loop(0, chunks, chunk_body, 0)

    o = o_scr[...] * jax.nn.silu(qz[:, D_MODEL:])
    out = _dot(o.astype(_BF), wout_ref[...])
    r = DEEPNORM_ALPHA * x + gate_ref[0] * out
    y_ref[0] = _layer_norm(r, lng_ref[...], lnb_ref[...])


def _attn_layer(x, ada, win, wout, sinks, tables, krep, vrep, lng, lnb, banded):
    bsz, seq, _ = x.shape
    rows = min(ATTN_ROWS, seq)
    keys = krep.shape[1]
    shift, scale, gate = (a.reshape(bsz, 1, D_MODEL) for a in ada)
    vec = lambda a: a.reshape(1, D_MODEL)
    row_spec = pl.BlockSpec((1, rows, D_MODEL), lambda b, i: (b, i, 0))
    ada_spec = pl.BlockSpec((1, 1, D_MODEL), lambda b, i: (b, 0, 0))
    tab_spec = pl.BlockSpec((rows, LANES), lambda b, i: (i, 0))
    kv_spec = pl.BlockSpec((1, keys, N_KV_HEADS * HEAD_TILE), lambda b, i: (b, 0, 0))
    kern = functools.partial(_attn_layer_kernel, rows=rows, banded=banded)
    return pl.pallas_call(
        kern,
        grid=(bsz, seq // rows),
        in_specs=[pl.BlockSpec(memory_space=pltpu.SMEM),
                  row_spec, ada_spec, ada_spec, ada_spec,
                  _const_spec((D_MODEL, 2 * D_MODEL)), _const_spec((D_MODEL, D_MODEL)),
                  tab_spec, tab_spec, tab_spec, kv_spec, kv_spec,
                  _const_spec((1, D_MODEL)), _const_spec((1, D_MODEL))],
        out_specs=row_spec,
        out_shape=jax.ShapeDtypeStruct((bsz, seq, D_MODEL), _F32),
        scratch_shapes=[pltpu.VMEM((rows, D_MODEL), _F32), pltpu.VMEM((rows, D_MODEL), _F32)],
        compiler_params=pltpu.CompilerParams(dimension_semantics=("arbitrary", "arbitrary"),
                                             vmem_limit_bytes=VMEM_LIMIT),
        name="attn_layer",
    )(sinks, x, shift, scale, gate, win, wout, *tables, krep, vrep, vec(lng), vec(lnb))


def _run_trunk(x, ada_all, pos, h0, cache_k, cache_v, p):
    bsz, seq, _ = x.shape
    ada = lambda l: tuple(ada_all[l][:, k * D_MODEL:(k + 1) * D_MODEL] for k in range(3))
    x_tb = x.transpose(1, 0, 2).reshape(seq * bsz, D_MODEL)
    states = []
    for la in range(N_A_LAYERS):
        x_tb, h_last = _s5_layer(x_tb, bsz, ada(la), h0[la], p['win_a'][la], p['bs'][la], p['cs'][la],
                                 p['ar'][la], p['ai'][la], p['ssm_d'][la], p['w_glu'][la], p['b_glu'][la],
                                 p['w_out_a'][la], p['ln_g'][la], p['ln_b'][la])
        states.append(_strips_to_state(h_last))
    x = x_tb.reshape(seq, bsz, D_MODEL).transpose(1, 0, 2)
    tables = _rope_tables(pos)
    k_new, v_new, krep, vrep = _shared_kv(x, p['w_kv'], tables)
    if cache_k is not None:
        rep = lambda c: jnp.repeat(c.astype(_BF), Q_PER_KV, axis=2).reshape(bsz, c.shape[1], N_KV_HEADS * HEAD_TILE)
        krep = jnp.concatenate([rep(cache_k), krep], axis=1)
        vrep = jnp.concatenate([rep(cache_v), vrep], axis=1)
    for lb in range(N_B_LAYERS):
        layer = N_A_LAYERS + lb
        x = _attn_layer(x, ada(layer), p['win_b'][lb], p['w_out_b'][lb], p['attn_sinks'][lb], tables, krep, vrep,
                        p['ln_g'][layer], p['ln_b'][layer], banded=cache_k is None)
    k4 = k_new.reshape(bsz, seq, N_KV_HEADS, HEAD_DIM)
    v4 = v_new.reshape(bsz, seq, N_KV_HEADS, HEAD_DIM)
    return x, jnp.stack(states), k4, v4


def kernel(x_prompt, x_sample, state_ssm, cache_k, cache_v, c_prompt, c_sample, w_ada, b_ada, ln_g, ln_b, w_in_a,
           ssm_a_re, ssm_a_im, ssm_b_re, ssm_b_im, ssm_c_re, ssm_c_im, ssm_d, ssm_log_dt, w_glu, b_glu, w_out_a,
           w_kv, w_in_b, attn_sinks, w_out_b):
    n_prompt, n_sample = x_prompt.shape[0], x_sample.shape[0]
    ada_all = _ada_params(jnp.concatenate([c_prompt, c_sample], axis=0), w_ada, b_ada)
    ab_re, ab_im, bb_re, bb_im = _s5_prep(ssm_a_re, ssm_a_im, ssm_log_dt, ssm_b_re, ssm_b_im)
    bs, cs, ar, ai = _s5_matrices(ab_re, ab_im, bb_re, bb_im, ssm_c_re, ssm_c_im)
    p = dict(win_a=w_in_a.astype(_BF), bs=bs, cs=cs, ar=ar, ai=ai, ssm_d=ssm_d, w_glu=w_glu.astype(_BF),
             b_glu=b_glu, w_out_a=w_out_a.astype(_BF), ln_g=ln_g, ln_b=ln_b, w_kv=w_kv.astype(_BF),
             win_b=w_in_b.astype(_BF), w_out_b=w_out_b.astype(_BF), attn_sinks=attn_sinks)

    pos_prompt = jnp.arange(x_prompt.shape[1], dtype=jnp.int32)
    pos_sample = PAST_LEN + jnp.arange(x_sample.shape[1], dtype=jnp.int32)
    h0_prompt = jnp.zeros((N_A_LAYERS, N_STRIPS, n_prompt, STRIP_W), _F32)
    h0_sample = jnp.stack([_state_to_strips(state_ssm[la]) for la in range(N_A_LAYERS)])

    y_p, ssm_p, k_p, v_p = _run_trunk(x_prompt, ada_all[:, :n_prompt], pos_prompt, h0_prompt, None, None, p)
    y_s, ssm_s, k_s, v_s = _run_trunk(x_sample, ada_all[:, n_prompt:], pos_sample, h0_sample, cache_k, cache_v, p)
    rows = min(WINDOW, x_prompt.shape[1])
    return (y_p, y_s, ssm_p, k_p[:, -rows:], v_p[:, -rows:], ssm_s, k_s, v_s)
```

```python
import functools
import math

import jax
import jax.numpy as jnp
from jax import lax
from jax.experimental import pallas as pl
from jax.experimental.pallas import tpu as pltpu

D_MODEL = 1024
DEPTH = 4
CHUNK = 64
N_A_LAYERS = 2
N_B_LAYERS = 2
SSM_GROUP_CH = 16
SSM_GROUPS = 64
SSM_STATE = 64
HEAD_DIM = 64
N_HEADS = 16
N_KV_HEADS = 4
Q_PER_KV = 4
KV_WIDTH = N_KV_HEADS * HEAD_DIM
WINDOW = 128
WINDOW_CHUNKS = WINDOW // CHUNK
PAST_LEN = 1024
ROPE_THETA = 10000.0
NEG_INF = -1e30
DEEPNORM_ALPHA = (2.0 * DEPTH) ** 0.25
LN_EPS = 1e-5

LANES = 128
STRIP_GROUPS = LANES // SSM_GROUP_CH
N_STRIPS = SSM_GROUPS // STRIP_GROUPS
STRIP_HALF = STRIP_GROUPS * SSM_STATE
STRIP_W = 2 * STRIP_HALF
SCAN_W = 256
S5_ROWS = 512
KEYS = (WINDOW_CHUNKS + 1) * CHUNK
PAIR_KEYS = KEYS + CHUNK
ATTN_ROWS = 512
KV_ROWS = 512
MASK_PASS = 3.0e38
VMEM_LIMIT = 56 * 1024 * 1024

_BF = jnp.bfloat16
_F32 = jnp.float32


def _dot(a, b):
    return jnp.dot(a, b, preferred_element_type=_F32)


def _const_spec(shape):
    nd = len(shape)
    return pl.BlockSpec(shape, lambda *_: (0,) * nd, pipeline_mode=pl.Buffered(1))


def _layer_norm(r, g, b):
    mu = jnp.mean(r, axis=-1, keepdims=True)
    d = r - mu
    var = jnp.mean(d * d, axis=-1, keepdims=True)
    return d * lax.rsqrt(var + LN_EPS) * g + b


def _ada_kernel(c_ref, w_ref, b_ref, o_ref):
    c = c_ref[...]
    o_ref[0] = _dot(jax.nn.silu(c).astype(_BF), w_ref[0]) + b_ref[0]


def _ada_params(c_all, w_ada, b_ada):
    n = c_all.shape[0]
    return pl.pallas_call(
        _ada_kernel,
        grid=(DEPTH, 3),
        in_specs=[pl.BlockSpec((n, D_MODEL), lambda l, j: (0, 0)),
                  pl.BlockSpec((1, D_MODEL, D_MODEL), lambda l, j: (l, 0, j)),
                  pl.BlockSpec((1, 1, D_MODEL), lambda l, j: (l, 0, j))],
        out_specs=pl.BlockSpec((1, n, D_MODEL), lambda l, j: (l, 0, j)),
        out_shape=jax.ShapeDtypeStruct((DEPTH, n, 3 * D_MODEL), _F32),
        name="ada_params",
    )(c_all, w_ada.astype(_BF), b_ada.reshape(DEPTH, 1, 3 * D_MODEL))


def _s5_prep_kernel(are_ref, aim_ref, ldt_ref, bre_ref, bim_ref, oar_ref, oai_ref, obr_ref, obi_ref):
    a_re, a_im = are_ref[0], aim_ref[0]
    dt = jnp.exp(ldt_ref[0])
    mag = jnp.exp(a_re * dt)
    ab_re = mag * jnp.cos(a_im * dt)
    ab_im = mag * jnp.sin(a_im * dt)
    nr, ni = ab_re - 1.0, ab_im
    den = a_re * a_re + a_im * a_im
    f_re = (nr * a_re + ni * a_im) / den
    f_im = (ni * a_re - nr * a_im) / den
    b_re, b_im = bre_ref[0], bim_ref[0]
    oar_ref[0] = ab_re
    oai_ref[0] = ab_im
    obr_ref[0] = f_re * b_re - f_im * b_im
    obi_ref[0] = f_re * b_im + f_im * b_re


def _s5_prep(a_re, a_im, log_dt, b_re, b_im):
    w = SSM_STATE * SSM_GROUP_CH
    rep = lambda a: jnp.repeat(a, SSM_GROUP_CH, axis=-1)
    ldt = jnp.broadcast_to(log_dt[:, :, None], (N_A_LAYERS, SSM_GROUPS, w))
    spec = pl.BlockSpec((1, SSM_GROUPS, w), lambda l: (l, 0, 0))
    shp = jax.ShapeDtypeStruct((N_A_LAYERS, SSM_GROUPS, w), _F32)
    ab_re, ab_im, bb_re, bb_im = pl.pallas_call(
        _s5_prep_kernel, grid=(N_A_LAYERS,), in_specs=[spec] * 5, out_specs=[spec] * 4,
        out_shape=[shp] * 4, name="s5_prep",
    )(rep(a_re), rep(a_im), ldt, b_re.reshape(N_A_LAYERS, SSM_GROUPS, w), b_im.reshape(N_A_LAYERS, SSM_GROUPS, w))
    gp = (N_A_LAYERS, SSM_GROUPS, SSM_STATE, SSM_GROUP_CH)
    return (ab_re.reshape(gp)[..., 0], ab_im.reshape(gp)[..., 0], bb_re.reshape(gp), bb_im.reshape(gp))


def _s5_matrices(ab_re, ab_im, bb_re, bb_im, c_re, c_im):
    eye = jnp.eye(STRIP_GROUPS, dtype=_F32)
    nl = N_A_LAYERS
    bb = jnp.stack([bb_re, bb_im], axis=1).reshape(nl, 2, N_STRIPS, STRIP_GROUPS, SSM_STATE, SSM_GROUP_CH)
    bs = jnp.einsum('lrjgpc,gh->ljgcrhp', bb, eye).reshape(nl, N_STRIPS, LANES, STRIP_W)
    cc = jnp.stack([c_re, -c_im], axis=1).reshape(nl, 2, N_STRIPS, STRIP_GROUPS, SSM_GROUP_CH, SSM_STATE)
    cs = jnp.einsum('lrjgcp,gh->ljrhpgc', cc, eye).reshape(nl, N_STRIPS, STRIP_W, LANES)
    ar = ab_re.reshape(nl, N_STRIPS, 1, STRIP_HALF)
    ai = ab_im.reshape(nl, N_STRIPS, 1, STRIP_HALF)
    return bs.astype(_BF), cs.astype(_BF), ar, ai


def _state_to_strips(state):
    nb = state.shape[0]
    s = state.reshape(nb, N_STRIPS, STRIP_GROUPS, SSM_STATE, 2)
    return s.transpose(1, 0, 4, 2, 3).reshape(N_STRIPS, nb, STRIP_W)


def _strips_to_state(h):
    nb = h.shape[1]
    s = h.reshape(N_STRIPS, nb, 2, STRIP_GROUPS, SSM_STATE)
    return s.transpose(1, 0, 3, 4, 2).reshape(nb, SSM_GROUPS, SSM_STATE, 2)


def _s5_layer_kernel(x_ref, shift_ref, scale_ref, gate_ref, h0_ref, win_ref, bs_ref, cs_ref, ar_ref, ai_ref,
                     d_ref, wglu_ref, bglu_ref, wout_ref, lng_ref, lnb_ref,
                     y_ref, hout_ref,
                     h_scr, u_scr, z_scr, bu_scr, yy_scr, *, nb, steps):
    rows = nb * steps
    i = pl.program_id(0)

    @pl.when(i == 0)
    def _():
        h_scr[...] = h0_ref[...]

    x3 = x_ref[...].reshape(steps, nb, D_MODEL)
    hmod = (x3 * (1.0 + scale_ref[...])[None] + shift_ref[...][None]).reshape(rows, D_MODEL)
    uz = _dot(hmod.astype(_BF), win_ref[...])
    for j in range(N_STRIPS):
        u_scr[j] = uz[:, j * LANES:(j + 1) * LANES]
    z_scr[...] = uz[:, D_MODEL:]

    def expand(j):
        bu_scr[j % 2] = _dot(u_scr[j].astype(_BF), bs_ref[j])

    def scan(j):
        buf = j % 2
        a_re_row, a_im_row = ar_ref[j], ai_ref[j]
        for s in range(STRIP_HALF // SCAN_W):
            re = slice(s * SCAN_W, (s + 1) * SCAN_W)
            im = slice(STRIP_HALF + s * SCAN_W, STRIP_HALF + (s + 1) * SCAN_W)
            a_re = jnp.broadcast_to(a_re_row[:, re], (nb, SCAN_W))
            a_im = jnp.broadcast_to(a_im_row[:, re], (nb, SCAN_W))
            h_re, h_im = h_scr[j, :, re], h_scr[j, :, im]
            for t in range(steps):
                now = slice(t * nb, (t + 1) * nb)
                n_re = a_re * h_re - a_im * h_im + bu_scr[buf, now, re]
                n_im = a_re * h_im + a_im * h_re + bu_scr[buf, now, im]
                bu_scr[buf, now, re] = n_re
                bu_scr[buf, now, im] = n_im
                h_re, h_im = n_re, n_im
            h_scr[j, :, re] = h_re
            h_scr[j, :, im] = h_im

    def project(j):
        yy_scr[j] = _dot(bu_scr[j % 2].astype(_BF), cs_ref[j]) + d_ref[j] * u_scr[j]

    expand(0)
    for j in range(N_STRIPS):
        if j + 1 < N_STRIPS:
            expand(j + 1)
        scan(j)
        project(j)

    y = jnp.concatenate([yy_scr[j] for j in range(N_STRIPS)], axis=1)
    g = jax.nn.gelu(y)
    y = g * jax.nn.sigmoid(_dot(g.astype(_BF), wglu_ref[...]) + bglu_ref[...])
    y = y * jax.nn.silu(z_scr[...])
    out = _dot(y.astype(_BF), wout_ref[...]).reshape(steps, nb, D_MODEL)
    r = (DEEPNORM_ALPHA * x3 + gate_ref[...][None] * out).reshape(rows, D_MODEL)
    y_ref[...] = _layer_norm(r, lng_ref[...], lnb_ref[...])

    @pl.when(i == pl.num_programs(0) - 1)
    def _():
        hout_ref[...] = h_scr[...]


def _s5_layer(x_tb, nb, ada, h0, win, bs, cs, ar, ai, d, wglu, bglu, wout, lng, lnb):
    total = x_tb.shape[0]
    rows = min(S5_ROWS, total)
    steps = rows // nb
    shift, scale, gate = ada
    row_spec = pl.BlockSpec((rows, D_MODEL), lambda i: (i, 0))
    vec = lambda a: a.reshape(1, D_MODEL)
    kern = functools.partial(_s5_layer_kernel, nb=nb, steps=steps)
    return pl.pallas_call(
        kern,
        grid=(total // rows,),
        in_specs=[row_spec,
                  _const_spec((nb, D_MODEL)), _const_spec((nb, D_MODEL)), _const_spec((nb, D_MODEL)),
                  _const_spec((N_STRIPS, nb, STRIP_W)),
                  _const_spec((D_MODEL, 2 * D_MODEL)),
                  _const_spec((N_STRIPS, LANES, STRIP_W)),
                  _const_spec((N_STRIPS, STRIP_W, LANES)),
                  _const_spec((N_STRIPS, 1, STRIP_HALF)), _const_spec((N_STRIPS, 1, STRIP_HALF)),
                  _const_spec((N_STRIPS, 1, LANES)),
                  _const_spec((D_MODEL, D_MODEL)), _const_spec((1, D_MODEL)),
                  _const_spec((D_MODEL, D_MODEL)),
                  _const_spec((1, D_MODEL)), _const_spec((1, D_MODEL))],
        out_specs=[row_spec, pl.BlockSpec((N_STRIPS, nb, STRIP_W), lambda i: (0, 0, 0))],
        out_shape=[jax.ShapeDtypeStruct((total, D_MODEL), _F32),
                   jax.ShapeDtypeStruct((N_STRIPS, nb, STRIP_W), _F32)],
        scratch_shapes=[pltpu.VMEM((N_STRIPS, nb, STRIP_W), _F32),
                        pltpu.VMEM((N_STRIPS, rows, LANES), _F32),
                        pltpu.VMEM((rows, D_MODEL), _F32),
                        pltpu.VMEM((2, rows, STRIP_W), _F32),
                        pltpu.VMEM((N_STRIPS, rows, LANES), _F32)],
        compiler_params=pltpu.CompilerParams(dimension_semantics=("arbitrary",), vmem_limit_bytes=VMEM_LIMIT),
        name="s5_layer",
    )(x_tb, shift, scale, gate, h0, win, bs, cs, ar, ai, d.reshape(N_STRIPS, 1, LANES),
      wglu, vec(bglu), wout, vec(lng), vec(lnb))


def _rope_angles(pos):
    half = HEAD_DIM // 2
    inv_freq = jnp.power(ROPE_THETA, -jnp.arange(half, dtype=_F32) / half)
    ang = pos.astype(_F32)[:, None] * inv_freq[None, :]
    return jnp.cos(ang), jnp.sin(ang)


def _rope_tables(pos):
    cos, sin = _rope_angles(pos)
    zero = jnp.zeros_like(sin)
    cos_t = jnp.tile(cos, (1, LANES // (HEAD_DIM // 2)))
    sin_lo = jnp.tile(jnp.concatenate([-sin, zero], axis=1), (1, LANES // HEAD_DIM))
    sin_hi = jnp.tile(jnp.concatenate([zero, sin], axis=1), (1, LANES // HEAD_DIM))
    return cos_t, sin_lo, sin_hi


def _rope_tile(x, cos_t, sin_lo, sin_hi):
    half = HEAD_DIM // 2
    return x * cos_t + pltpu.roll(x, LANES - half, 1) * sin_lo + pltpu.roll(x, half, 1) * sin_hi


def _kv_kernel(x_ref, w_ref, cos_ref, slo_ref, shi_ref, k_ref, v_ref, kbf_ref, *vt_ref):
    kv = _dot(x_ref[0].astype(_BF), w_ref[...])
    cos_t, sin_lo, sin_hi = cos_ref[...], slo_ref[...], shi_ref[...]
    for t in range(KV_WIDTH // LANES):
        tile = _rope_tile(kv[:, t * LANES:(t + 1) * LANES], cos_t, sin_lo, sin_hi)
        k_ref[0, :, t * LANES:(t + 1) * LANES] = tile
        kbf_ref[0, :, t * LANES:(t + 1) * LANES] = tile.astype(_BF)
    v = kv[:, KV_WIDTH:]
    v_ref[0] = v
    if vt_ref:
        for blk in range(v.shape[0] // LANES):
            vt_ref[0][0, blk] = v[blk * LANES:(blk + 1) * LANES, :].T.astype(_BF)


def _shared_kv(x, w_kv, tables, transposed_v):
    bsz, seq, _ = x.shape
    rows = min(KV_ROWS, seq)
    tab_spec = pl.BlockSpec((rows, LANES), lambda b, i: (i, 0))
    kv_spec = pl.BlockSpec((1, rows, KV_WIDTH), lambda b, i: (b, i, 0))
    out_specs = [kv_spec] * 3
    out_shape = ([jax.ShapeDtypeStruct((bsz, seq, KV_WIDTH), _F32)] * 2
                 + [jax.ShapeDtypeStruct((bsz, seq, KV_WIDTH), _BF)])
    if transposed_v:
        out_specs.append(pl.BlockSpec((1, rows // LANES, KV_WIDTH, LANES), lambda b, i: (b, i, 0, 0)))
        out_shape.append(jax.ShapeDtypeStruct((bsz, seq // LANES, KV_WIDTH, LANES), _BF))
    return pl.pallas_call(
        _kv_kernel,
        grid=(bsz, seq // rows),
        in_specs=[pl.BlockSpec((1, rows, D_MODEL), lambda b, i: (b, i, 0)),
                  _const_spec((D_MODEL, 2 * KV_WIDTH)), tab_spec, tab_spec, tab_spec],
        out_specs=out_specs, out_shape=out_shape,
        compiler_params=pltpu.CompilerParams(dimension_semantics=("arbitrary", "arbitrary"),
                                             vmem_limit_bytes=VMEM_LIMIT),
        name="shared_kv",
    )(x, w_kv, *tables)


def _reduce_rows(x, pair_op, final_op):
    sub = 8
    parts = [x[i * sub:(i + 1) * sub] for i in range(x.shape[0] // sub)]
    while len(parts) > 1:
        parts = [pair_op(parts[i], parts[i + 1]) for i in range(0, len(parts), 2)]
    return final_op(parts[0], axis=0, keepdims=True)


def _attn_layer_kernel(sinks_ref, x_ref, shift_ref, scale_ref, gate_ref, wint_ref, woutt_ref, cos_ref, sin_ref,
                       k_ref, vt_ref, lng_ref, lnb_ref, y_ref, qt_scr, gt_scr, ot_scr, cap_scr, *, rows, banded):
    x = x_ref[0]
    hmod = (x * (1.0 + scale_ref[0]) + shift_ref[0]).astype(_BF)
    qzt = lax.dot_general(wint_ref[...], hmod, (((1,), (1,)), ((), ())), preferred_element_type=_F32)
    cos_t, sin_t = cos_ref[...], sin_ref[...]
    half = HEAD_DIM // 2
    qk_scale = HEAD_DIM ** -0.5
    for h in range(N_HEADS):
        x1, x2 = qzt[h * HEAD_DIM:h * HEAD_DIM + half], qzt[h * HEAD_DIM + half:(h + 1) * HEAD_DIM]
        qt_scr[h * HEAD_DIM:h * HEAD_DIM + half] = ((x1 * cos_t - x2 * sin_t) * qk_scale).astype(_BF)
        qt_scr[h * HEAD_DIM + half:(h + 1) * HEAD_DIM] = ((x2 * cos_t + x1 * sin_t) * qk_scale).astype(_BF)
    gt_scr[...] = jax.nn.silu(qzt[D_MODEL:])

    key_blk = lax.broadcasted_iota(jnp.int32, (PAIR_KEYS, LANES), 0) // CHUNK
    qry_blk = lax.broadcasted_iota(jnp.int32, (PAIR_KEYS, LANES), 1) // CHUNK
    no_q = jnp.zeros((HEAD_DIM, LANES), _BF)
    pairs = []
    for a in range(rows // LANES):
        if banded:
            qc0 = pl.program_id(1) * (rows // CHUNK) + 2 * a
            kc0 = jnp.maximum(qc0 - WINDOW_CHUNKS, 0)
            delta0 = kc0 - qc0
            kstart = pl.multiple_of(kc0 * CHUNK, LANES)
            vblk = kc0 // 2
        else:
            delta0, kstart, vblk = -WINDOW_CHUNKS, 0, 0
        rel = delta0 + key_blk - qry_blk
        cap_scr[a] = jnp.where((rel >= -WINDOW_CHUNKS) & (rel <= 0), MASK_PASS, NEG_INF)
        pairs.append((kstart, vblk))

    def unit_heads(kv, gp):
        return (kv * Q_PER_KV + 2 * gp, kv * Q_PER_KV + 2 * gp + 1)

    def scores(a, kv, gp):
        lanes = slice(a * LANES, (a + 1) * LANES)
        k_tile = k_ref[0, pl.ds(pairs[a][0], PAIR_KEYS), (kv // 2) * LANES:(kv // 2 + 1) * LANES]
        w_parts = []
        for h in unit_heads(kv, gp):
            q_h = qt_scr[h * HEAD_DIM:(h + 1) * HEAD_DIM, lanes]
            w_parts.append(jnp.concatenate([q_h, no_q] if kv % 2 == 0 else [no_q, q_h], axis=0))
        return _dot(k_tile, jnp.concatenate(w_parts, axis=1))

    def softmax_values(a, kv, gp, s):
        e_parts, inv_den = [], []
        for u, h in enumerate(unit_heads(kv, gp)):
            s_h = jnp.minimum(s[:, u * LANES:(u + 1) * LANES], cap_scr[a])
            sink = sinks_ref[h]
            m = jnp.maximum(_reduce_rows(s_h, jnp.maximum, jnp.max), sink)
            e = jnp.exp(s_h - m)
            inv_den.append(1.0 / (_reduce_rows(e, jnp.add, jnp.sum) + jnp.exp(sink - m)))
            e_parts.append(e.astype(_BF))
        vblk = pairs[a][1]
        vt_kv = jnp.concatenate([vt_ref[0, vblk, kv * HEAD_DIM:(kv + 1) * HEAD_DIM, :],
                                 vt_ref[0, vblk + 1, kv * HEAD_DIM:(kv + 1) * HEAD_DIM, :]], axis=1)
        return _dot(vt_kv, jnp.concatenate(e_parts, axis=1)), inv_den

    def finish(a, kv, gp, ot, inv_den):
        lanes = slice(a * LANES, (a + 1) * LANES)
        for u, h in enumerate(unit_heads(kv, gp)):
            o_h = ot[:, u * LANES:(u + 1) * LANES] * inv_den[u] * gt_scr[h * HEAD_DIM:(h + 1) * HEAD_DIM, lanes]
            ot_scr[h * HEAD_DIM:(h + 1) * HEAD_DIM, lanes] = o_h.astype(_BF)

    units = [(a, kv, gp) for a in range(rows // LANES) for kv in range(N_KV_HEADS) for gp in range(Q_PER_KV // 2)]
    s_next, unfinished = scores(*units[0]), None
    for i, unit in enumerate(units):
        s_cur = s_next
        if i + 1 < len(units):
            s_next = scores(*units[i + 1])
        ot, inv_den = softmax_values(*unit, s_cur)
        if unfinished is not None:
            finish(*unfinished)
        unfinished = (*unit, ot, inv_den)
    finish(*unfinished)

    out = _dot(woutt_ref[...], ot_scr[...]).T
    r = DEEPNORM_ALPHA * x + gate_ref[0] * out
    y_ref[0] = _layer_norm(r, lng_ref[...], lnb_ref[...])


def _attn_layer(x, ada, win_t, wout_t, sinks, tables_t, k_bf, vt, lng, lnb, banded):
    bsz, seq, _ = x.shape
    rows = min(ATTN_ROWS, seq)
    keys = k_bf.shape[1]
    shift, scale, gate = (a.reshape(bsz, 1, D_MODEL) for a in ada)
    vec = lambda a: a.reshape(1, D_MODEL)
    row_spec = pl.BlockSpec((1, rows, D_MODEL), lambda b, i: (b, i, 0))
    ada_spec = pl.BlockSpec((1, 1, D_MODEL), lambda b, i: (b, 0, 0))
    tab_spec = pl.BlockSpec((HEAD_DIM // 2, rows), lambda b, i: (0, i))
    kern = functools.partial(_attn_layer_kernel, rows=rows, banded=banded)
    return pl.pallas_call(
        kern,
        grid=(bsz, seq // rows),
        in_specs=[pl.BlockSpec(memory_space=pltpu.SMEM),
                  row_spec, ada_spec, ada_spec, ada_spec,
                  _const_spec((2 * D_MODEL, D_MODEL)), _const_spec((D_MODEL, D_MODEL)),
                  tab_spec, tab_spec,
                  pl.BlockSpec((1, keys, KV_WIDTH), lambda b, i: (b, 0, 0)),
                  pl.BlockSpec((1, keys // LANES, KV_WIDTH, LANES), lambda b, i: (b, 0, 0, 0)),
                  _const_spec((1, D_MODEL)), _const_spec((1, D_MODEL))],
        out_specs=row_spec,
        out_shape=jax.ShapeDtypeStruct((bsz, seq, D_MODEL), _F32),
        scratch_shapes=[pltpu.VMEM((D_MODEL, rows), _BF), pltpu.VMEM((D_MODEL, rows), _F32),
                        pltpu.VMEM((D_MODEL, rows), _BF), pltpu.VMEM((rows // LANES, PAIR_KEYS, LANES), _F32)],
        compiler_params=pltpu.CompilerParams(dimension_semantics=("arbitrary", "arbitrary"),
                                             vmem_limit_bytes=VMEM_LIMIT),
        name="attn_layer",
    )(sinks, x, shift, scale, gate, win_t, wout_t, *tables_t, k_bf, vt, vec(lng), vec(lnb))


def _run_trunk(x, ada_all, pos, h0, cache_k, cache_v, p):
    bsz, seq, _ = x.shape
    ada = lambda l: tuple(ada_all[l][:, k * D_MODEL:(k + 1) * D_MODEL] for k in range(3))
    x_tb = x.transpose(1, 0, 2).reshape(seq * bsz, D_MODEL)
    states = []
    for la in range(N_A_LAYERS):
        x_tb, h_last = _s5_layer(x_tb, bsz, ada(la), h0[la], p['win_a'][la], p['bs'][la], p['cs'][la],
                                 p['ar'][la], p['ai'][la], p['ssm_d'][la], p['w_glu'][la], p['b_glu'][la],
                                 p['w_out_a'][la], p['ln_g'][la], p['ln_b'][la])
        states.append(_strips_to_state(h_last))
    x = x_tb.reshape(seq, bsz, D_MODEL).transpose(1, 0, 2)
    banded = cache_k is None
    outs = _shared_kv(x, p['w_kv'], _rope_tables(pos), transposed_v=banded)
    k_new, v_new, k_bf = outs[:3]
    if banded:
        vt = outs[3]
        q_pos = pos
    else:
        pad = LANES - seq
        flat = lambda c: c.reshape(bsz, c.shape[1], KV_WIDTH)
        k_bf = jnp.concatenate([flat(cache_k).astype(_BF), k_bf, jnp.zeros((bsz, pad, KV_WIDTH), _BF)], axis=1)
        v_all = jnp.concatenate([flat(cache_v), v_new, jnp.zeros((bsz, pad, KV_WIDTH), _F32)], axis=1)
        vt = v_all.reshape(bsz, PAIR_KEYS // LANES, LANES, KV_WIDTH).transpose(0, 1, 3, 2).astype(_BF)
        x = jnp.pad(x, ((0, 0), (0, pad), (0, 0)))
        q_pos = pos[0] + jnp.arange(LANES, dtype=jnp.int32)
    cos, sin = _rope_angles(q_pos)
    tables_t = (cos.T, sin.T)
    for lb in range(N_B_LAYERS):
        layer = N_A_LAYERS + lb
        x = _attn_layer(x, ada(layer), p['win_b_t'][lb], p['w_out_b_t'][lb], p['attn_sinks'][lb], tables_t, k_bf, vt,
                        p['ln_g'][layer], p['ln_b'][layer], banded=banded)
    k4 = k_new.reshape(bsz, seq, N_KV_HEADS, HEAD_DIM)
    v4 = v_new.reshape(bsz, seq, N_KV_HEADS, HEAD_DIM)
    return x[:, :seq], jnp.stack(states), k4, v4


def kernel(x_prompt, x_sample, state_ssm, cache_k, cache_v, c_prompt, c_sample, w_ada, b_ada, ln_g, ln_b, w_in_a,
           ssm_a_re, ssm_a_im, ssm_b_re, ssm_b_im, ssm_c_re, ssm_c_im, ssm_d, ssm_log_dt, w_glu, b_glu, w_out_a,
           w_kv, w_in_b, attn_sinks, w_out_b):
    n_prompt, n_sample = x_prompt.shape[0], x_sample.shape[0]
    ada_all = _ada_params(jnp.concatenate([c_prompt, c_sample], axis=0), w_ada, b_ada)
    ab_re, ab_im, bb_re, bb_im = _s5_prep(ssm_a_re, ssm_a_im, ssm_log_dt, ssm_b_re, ssm_b_im)
    bs, cs, ar, ai = _s5_matrices(ab_re, ab_im, bb_re, bb_im, ssm_c_re, ssm_c_im)
    p = dict(win_a=w_in_a.astype(_BF), bs=bs, cs=cs, ar=ar, ai=ai, ssm_d=ssm_d, w_glu=w_glu.astype(_BF),
             b_glu=b_glu, w_out_a=w_out_a.astype(_BF), ln_g=ln_g, ln_b=ln_b, w_kv=w_kv.astype(_BF),
             win_b_t=w_in_b.astype(_BF).transpose(0, 2, 1), w_out_b_t=w_out_b.astype(_BF).transpose(0, 2, 1),
             attn_sinks=attn_sinks)

    pos_prompt = jnp.arange(x_prompt.shape[1], dtype=jnp.int32)
    pos_sample = PAST_LEN + jnp.arange(x_sample.shape[1], dtype=jnp.int32)
    h0_prompt = jnp.zeros((N_A_LAYERS, N_STRIPS, n_prompt, STRIP_W), _F32)
    h0_sample = jnp.stack([_state_to_strips(state_ssm[la]) for la in range(N_A_LAYERS)])

    y_p, ssm_p, k_p, v_p = _run_trunk(x_prompt, ada_all[:, :n_prompt], pos_prompt, h0_prompt, None, None, p)
    y_s, ssm_s, k_s, v_s = _run_trunk(x_sample, ada_all[:, n_prompt:], pos_sample, h0_sample, cache_k, cache_v, p)
    rows = min(WINDOW, x_prompt.shape[1])
    return (y_p, y_s, ssm_p, k_p[:, -rows:], v_p[:, -rows:], ssm_s, k_s, v_s)
```

```python
import functools
import math

import jax
import jax.numpy as jnp
from jax import lax
from jax.experimental import pallas as pl
from jax.experimental.pallas import tpu as pltpu

D_MODEL = 1024
DEPTH = 4
CHUNK = 64
N_A_LAYERS = 2
N_B_LAYERS = 2
SSM_GROUP_CH = 16
SSM_GROUPS = 64
SSM_STATE = 64
HEAD_DIM = 64
N_HEADS = 16
N_KV_HEADS = 4
Q_PER_KV = 4
KV_WIDTH = N_KV_HEADS * HEAD_DIM
WINDOW = 128
WINDOW_CHUNKS = WINDOW // CHUNK
PAST_LEN = 1024
ROPE_THETA = 10000.0
NEG_INF = -1e30
DEEPNORM_ALPHA = (2.0 * DEPTH) ** 0.25
LN_EPS = 1e-5

LANES = 128
STRIP_GROUPS = LANES // SSM_GROUP_CH
N_STRIPS = SSM_GROUPS // STRIP_GROUPS
STRIP_HALF = STRIP_GROUPS * SSM_STATE
STRIP_W = 2 * STRIP_HALF
SCAN_W = 256
S5_ROWS = 512
KEYS = (WINDOW_CHUNKS + 1) * CHUNK
PAIR_KEYS = KEYS + CHUNK
ATTN_ROWS = 512
HEAD_TILE = Q_PER_KV * HEAD_DIM
IN_PIECE = 256
KV_ROWS = 512
MASK_PASS = 3.0e38
VMEM_LIMIT = 56 * 1024 * 1024

_BF = jnp.bfloat16
_F32 = jnp.float32


def _dot(a, b):
    return jnp.dot(a, b, preferred_element_type=_F32)


def _const_spec(shape):
    nd = len(shape)
    return pl.BlockSpec(shape, lambda *_: (0,) * nd, pipeline_mode=pl.Buffered(1))


def _layer_norm(r, g, b):
    mu = jnp.mean(r, axis=-1, keepdims=True)
    d = r - mu
    var = jnp.mean(d * d, axis=-1, keepdims=True)
    return d * lax.rsqrt(var + LN_EPS) * g + b


def _ada_kernel(c_ref, w_ref, b_ref, o_ref):
    c = c_ref[...]
    o_ref[0] = _dot(jax.nn.silu(c).astype(_BF), w_ref[0]) + b_ref[0]


def _ada_params(c_all, w_ada, b_ada):
    n = c_all.shape[0]
    return pl.pallas_call(
        _ada_kernel,
        grid=(DEPTH, 3),
        in_specs=[pl.BlockSpec((n, D_MODEL), lambda l, j: (0, 0)),
                  pl.BlockSpec((1, D_MODEL, D_MODEL), lambda l, j: (l, 0, j)),
                  pl.BlockSpec((1, 1, D_MODEL), lambda l, j: (l, 0, j))],
        out_specs=pl.BlockSpec((1, n, D_MODEL), lambda l, j: (l, 0, j)),
        out_shape=jax.ShapeDtypeStruct((DEPTH, n, 3 * D_MODEL), _F32),
        name="ada_params",
    )(c_all, w_ada.astype(_BF), b_ada.reshape(DEPTH, 1, 3 * D_MODEL))


def _s5_prep_kernel(are_ref, aim_ref, ldt_ref, bre_ref, bim_ref, oar_ref, oai_ref, obr_ref, obi_ref):
    a_re, a_im = are_ref[0], aim_ref[0]
    dt = jnp.exp(ldt_ref[0])
    mag = jnp.exp(a_re * dt)
    ab_re = mag * jnp.cos(a_im * dt)
    ab_im = mag * jnp.sin(a_im * dt)
    nr, ni = ab_re - 1.0, ab_im
    den = a_re * a_re + a_im * a_im
    f_re = (nr * a_re + ni * a_im) / den
    f_im = (ni * a_re - nr * a_im) / den
    b_re, b_im = bre_ref[0], bim_ref[0]
    oar_ref[0] = ab_re
    oai_ref[0] = ab_im
    obr_ref[0] = f_re * b_re - f_im * b_im
    obi_ref[0] = f_re * b_im + f_im * b_re


def _s5_prep(a_re, a_im, log_dt, b_re, b_im):
    w = SSM_STATE * SSM_GROUP_CH
    rep = lambda a: jnp.repeat(a, SSM_GROUP_CH, axis=-1)
    ldt = jnp.broadcast_to(log_dt[:, :, None], (N_A_LAYERS, SSM_GROUPS, w))
    spec = pl.BlockSpec((1, SSM_GROUPS, w), lambda l: (l, 0, 0))
    shp = jax.ShapeDtypeStruct((N_A_LAYERS, SSM_GROUPS, w), _F32)
    ab_re, ab_im, bb_re, bb_im = pl.pallas_call(
        _s5_prep_kernel, grid=(N_A_LAYERS,), in_specs=[spec] * 5, out_specs=[spec] * 4,
        out_shape=[shp] * 4, name="s5_prep",
    )(rep(a_re), rep(a_im), ldt, b_re.reshape(N_A_LAYERS, SSM_GROUPS, w), b_im.reshape(N_A_LAYERS, SSM_GROUPS, w))
    gp = (N_A_LAYERS, SSM_GROUPS, SSM_STATE, SSM_GROUP_CH)
    return (ab_re.reshape(gp)[..., 0], ab_im.reshape(gp)[..., 0], bb_re.reshape(gp), bb_im.reshape(gp))


def _s5_matrices(ab_re, ab_im, bb_re, bb_im, c_re, c_im):
    eye = jnp.eye(STRIP_GROUPS, dtype=_F32)
    nl = N_A_LAYERS
    bb = jnp.stack([bb_re, bb_im], axis=1).reshape(nl, 2, N_STRIPS, STRIP_GROUPS, SSM_STATE, SSM_GROUP_CH)
    bs = jnp.einsum('lrjgpc,gh->ljgcrhp', bb, eye).reshape(nl, N_STRIPS, LANES, STRIP_W)
    cc = jnp.stack([c_re, -c_im], axis=1).reshape(nl, 2, N_STRIPS, STRIP_GROUPS, SSM_GROUP_CH, SSM_STATE)
    cs = jnp.einsum('lrjgcp,gh->ljrhpgc', cc, eye).reshape(nl, N_STRIPS, STRIP_W, LANES)
    ar = ab_re.reshape(nl, N_STRIPS, 1, STRIP_HALF)
    ai = ab_im.reshape(nl, N_STRIPS, 1, STRIP_HALF)
    return bs.astype(_BF), cs.astype(_BF), ar, ai


def _state_to_strips(state):
    nb = state.shape[0]
    s = state.reshape(nb, N_STRIPS, STRIP_GROUPS, SSM_STATE, 2)
    return s.transpose(1, 0, 4, 2, 3).reshape(N_STRIPS, nb, STRIP_W)


def _strips_to_state(h):
    nb = h.shape[1]
    s = h.reshape(N_STRIPS, nb, 2, STRIP_GROUPS, SSM_STATE)
    return s.transpose(1, 0, 3, 4, 2).reshape(nb, SSM_GROUPS, SSM_STATE, 2)


def _s5_layer_kernel(x_ref, shift_ref, scale_ref, gate_ref, h0_ref, win_ref, bs_ref, cs_ref, ar_ref, ai_ref,
                     d_ref, wglu_ref, bglu_ref, wout_ref, lng_ref, lnb_ref,
                     y_ref, hout_ref,
                     h_scr, u_scr, z_scr, bu_scr, yy_scr, xt_scr, *, nb, steps, batch_major_in, batch_major_out):
    rows = nb * steps
    i = pl.program_id(0)

    @pl.when(i == 0)
    def _():
        h_scr[...] = h0_ref[...]

    lane_tiles = D_MODEL // LANES
    if batch_major_in:
        for b in range(nb):
            for c in range(lane_tiles):
                xt_scr[c, pl.ds(b, steps, stride=nb), :] = x_ref[b, :, c * LANES:(c + 1) * LANES]
        x_rows = jnp.concatenate([xt_scr[c] for c in range(lane_tiles)], axis=1)
    else:
        x_rows = x_ref[...]
    x3 = x_rows.reshape(steps, nb, D_MODEL)
    hmod = (x3 * (1.0 + scale_ref[...])[None] + shift_ref[...][None]).reshape(rows, D_MODEL)
    uz = _dot(hmod.astype(_BF), win_ref[...])
    for j in range(N_STRIPS):
        u_scr[j] = uz[:, j * LANES:(j + 1) * LANES]
    z_scr[...] = uz[:, D_MODEL:]

    def expand(j):
        bu_scr[j % 2] = _dot(u_scr[j].astype(_BF), bs_ref[j])

    def scan(j):
        buf = j % 2
        a_re_row, a_im_row = ar_ref[j], ai_ref[j]
        for s in range(STRIP_HALF // SCAN_W):
            re = slice(s * SCAN_W, (s + 1) * SCAN_W)
            im = slice(STRIP_HALF + s * SCAN_W, STRIP_HALF + (s + 1) * SCAN_W)
            a_re = jnp.broadcast_to(a_re_row[:, re], (nb, SCAN_W))
            a_im = jnp.broadcast_to(a_im_row[:, re], (nb, SCAN_W))
            h_re, h_im = h_scr[j, :, re], h_scr[j, :, im]
            for t in range(steps):
                now = slice(t * nb, (t + 1) * nb)
                n_re = a_re * h_re - a_im * h_im + bu_scr[buf, now, re]
                n_im = a_re * h_im + a_im * h_re + bu_scr[buf, now, im]
                bu_scr[buf, now, re] = n_re
                bu_scr[buf, now, im] = n_im
                h_re, h_im = n_re, n_im
            h_scr[j, :, re] = h_re
            h_scr[j, :, im] = h_im

    def project(j):
        yy_scr[j] = _dot(bu_scr[j % 2].astype(_BF), cs_ref[j]) + d_ref[j] * u_scr[j]

    expand(0)
    for j in range(N_STRIPS):
        if j + 1 < N_STRIPS:
            expand(j + 1)
        scan(j)
        project(j)

    y = jnp.concatenate([yy_scr[j] for j in range(N_STRIPS)], axis=1)
    g = jax.nn.gelu(y)
    y = g * jax.nn.sigmoid(_dot(g.astype(_BF), wglu_ref[...]) + bglu_ref[...])
    y = y * jax.nn.silu(z_scr[...])
    out = _dot(y.astype(_BF), wout_ref[...]).reshape(steps, nb, D_MODEL)
    r = (DEEPNORM_ALPHA * x3 + gate_ref[...][None] * out).reshape(rows, D_MODEL)
    y = _layer_norm(r, lng_ref[...], lnb_ref[...])
    if batch_major_out:
        for c in range(lane_tiles):
            xt_scr[c] = y[:, c * LANES:(c + 1) * LANES]
        for b in range(nb):
            for c in range(lane_tiles):
                y_ref[b, :, c * LANES:(c + 1) * LANES] = xt_scr[c, pl.ds(b, steps, stride=nb), :]
    else:
        y_ref[...] = y

    @pl.when(i == pl.num_programs(0) - 1)
    def _():
        hout_ref[...] = h_scr[...]


def _s5_layer(x, nb, seq, ada, h0, win, bs, cs, ar, ai, d, wglu, bglu, wout, lng, lnb, batch_major_in,
              batch_major_out):
    total = seq * nb
    rows = min(S5_ROWS, total)
    steps = rows // nb
    shift, scale, gate = ada
    tb_spec = pl.BlockSpec((rows, D_MODEL), lambda i: (i, 0))
    bt_spec = pl.BlockSpec((nb, steps, D_MODEL), lambda i: (0, i, 0))
    tb_shape = jax.ShapeDtypeStruct((total, D_MODEL), _F32)
    bt_shape = jax.ShapeDtypeStruct((nb, seq, D_MODEL), _F32)
    vec = lambda a: a.reshape(1, D_MODEL)
    kern = functools.partial(_s5_layer_kernel, nb=nb, steps=steps, batch_major_in=batch_major_in,
                             batch_major_out=batch_major_out)
    return pl.pallas_call(
        kern,
        grid=(total // rows,),
        in_specs=[bt_spec if batch_major_in else tb_spec,
                  _const_spec((nb, D_MODEL)), _const_spec((nb, D_MODEL)), _const_spec((nb, D_MODEL)),
                  _const_spec((N_STRIPS, nb, STRIP_W)),
                  _const_spec((D_MODEL, 2 * D_MODEL)),
                  _const_spec((N_STRIPS, LANES, STRIP_W)),
                  _const_spec((N_STRIPS, STRIP_W, LANES)),
                  _const_spec((N_STRIPS, 1, STRIP_HALF)), _const_spec((N_STRIPS, 1, STRIP_HALF)),
                  _const_spec((N_STRIPS, 1, LANES)),
                  _const_spec((D_MODEL, D_MODEL)), _const_spec((1, D_MODEL)),
                  _const_spec((D_MODEL, D_MODEL)),
                  _const_spec((1, D_MODEL)), _const_spec((1, D_MODEL))],
        out_specs=[bt_spec if batch_major_out else tb_spec,
                   pl.BlockSpec((N_STRIPS, nb, STRIP_W), lambda i: (0, 0, 0))],
        out_shape=[bt_shape if batch_major_out else tb_shape,
                   jax.ShapeDtypeStruct((N_STRIPS, nb, STRIP_W), _F32)],
        scratch_shapes=[pltpu.VMEM((N_STRIPS, nb, STRIP_W), _F32),
                        pltpu.VMEM((N_STRIPS, rows, LANES), _F32),
                        pltpu.VMEM((rows, D_MODEL), _F32),
                        pltpu.VMEM((2, rows, STRIP_W), _F32),
                        pltpu.VMEM((N_STRIPS, rows, LANES), _F32),
                        pltpu.VMEM((D_MODEL // LANES, rows, LANES), _F32)],
        compiler_params=pltpu.CompilerParams(dimension_semantics=("arbitrary",), vmem_limit_bytes=VMEM_LIMIT),
        name="s5_layer",
    )(x, shift, scale, gate, h0, win, bs, cs, ar, ai, d.reshape(N_STRIPS, 1, LANES),
      wglu, vec(bglu), wout, vec(lng), vec(lnb))


def _rope_angles(pos):
    half = HEAD_DIM // 2
    inv_freq = jnp.power(ROPE_THETA, -jnp.arange(half, dtype=_F32) / half)
    ang = pos.astype(_F32)[:, None] * inv_freq[None, :]
    return jnp.cos(ang), jnp.sin(ang)


def _rope_tables(pos):
    cos, sin = _rope_angles(pos)
    zero = jnp.zeros_like(sin)
    cos_t = jnp.tile(cos, (1, LANES // (HEAD_DIM // 2)))
    sin_lo = jnp.tile(jnp.concatenate([-sin, zero], axis=1), (1, LANES // HEAD_DIM))
    sin_hi = jnp.tile(jnp.concatenate([zero, sin], axis=1), (1, LANES // HEAD_DIM))
    return cos_t, sin_lo, sin_hi


def _rope_tile(x, cos_t, sin_lo, sin_hi):
    half = HEAD_DIM // 2
    return x * cos_t + pltpu.roll(x, LANES - half, 1) * sin_lo + pltpu.roll(x, half, 1) * sin_hi


def _kv_kernel(x_ref, w_ref, cos_ref, slo_ref, shi_ref, k_ref, v_ref, kbf_ref, *vt_ref):
    kv = _dot(x_ref[0].astype(_BF), w_ref[...])
    cos_t, sin_lo, sin_hi = cos_ref[...], slo_ref[...], shi_ref[...]
    for t in range(KV_WIDTH // LANES):
        tile = _rope_tile(kv[:, t * LANES:(t + 1) * LANES], cos_t, sin_lo, sin_hi)
        k_ref[0, :, t * LANES:(t + 1) * LANES] = tile
        kbf_ref[0, :, t * LANES:(t + 1) * LANES] = tile.astype(_BF)
    v = kv[:, KV_WIDTH:]
    v_ref[0] = v
    if vt_ref:
        for blk in range(v.shape[0] // LANES):
            vt_ref[0][0, blk] = v[blk * LANES:(blk + 1) * LANES, :].T.astype(_BF)


def _shared_kv(x, w_kv, tables, transposed_v):
    bsz, seq, _ = x.shape
    rows = min(KV_ROWS, seq)
    tab_spec = pl.BlockSpec((rows, LANES), lambda b, i: (i, 0))
    kv_spec = pl.BlockSpec((1, rows, KV_WIDTH), lambda b, i: (b, i, 0))
    out_specs = [kv_spec] * 3
    out_shape = ([jax.ShapeDtypeStruct((bsz, seq, KV_WIDTH), _F32)] * 2
                 + [jax.ShapeDtypeStruct((bsz, seq, KV_WIDTH), _BF)])
    if transposed_v:
        out_specs.append(pl.BlockSpec((1, rows // LANES, KV_WIDTH, LANES), lambda b, i: (b, i, 0, 0)))
        out_shape.append(jax.ShapeDtypeStruct((bsz, seq // LANES, KV_WIDTH, LANES), _BF))
    return pl.pallas_call(
        _kv_kernel,
        grid=(bsz, seq // rows),
        in_specs=[pl.BlockSpec((1, rows, D_MODEL), lambda b, i: (b, i, 0)),
                  _const_spec((D_MODEL, 2 * KV_WIDTH)), tab_spec, tab_spec, tab_spec],
        out_specs=out_specs, out_shape=out_shape,
        compiler_params=pltpu.CompilerParams(dimension_semantics=("arbitrary", "arbitrary"),
                                             vmem_limit_bytes=VMEM_LIMIT),
        name="shared_kv",
    )(x, w_kv, *tables)


def _reduce_rows(x, pair_op, final_op):
    sub = 8
    parts = [x[i * sub:(i + 1) * sub] for i in range(x.shape[0] // sub)]
    while len(parts) > 1:
        parts = [pair_op(parts[i], parts[i + 1]) for i in range(0, len(parts), 2)]
    return final_op(parts[0], axis=0, keepdims=True)


def _attn_layer_kernel(sinks_ref, x_ref, shift_ref, scale_ref, gate_ref, win_ref, wout_ref, cos_ref, sin_ref,
                       k_ref, vt_ref, lng_ref, lnb_ref, y_ref,
                       hm_scr, blk_scr, qt_scr, gt_scr, ot_scr, ob_scr, acc_scr, cap_scr, *, bb, rb, banded):
    rows = bb * rb
    n_pairs = rows // LANES
    pairs_per_stream = rb // LANES
    x = x_ref[...]
    hm_scr[...] = (x * (1.0 + scale_ref[...]) + shift_ref[...]).reshape(rows, D_MODEL).astype(_BF)
    cos_t, sin_t = cos_ref[...], sin_ref[...]
    half = HEAD_DIM // 2
    qk_scale = HEAD_DIM ** -0.5
    kt = IN_PIECE

    def in_piece(c, k):
        part = _dot(hm_scr[:, k * kt:(k + 1) * kt], win_ref[k * kt:(k + 1) * kt, c * HEAD_TILE:(c + 1) * HEAD_TILE])
        if k == 0:
            blk_scr[c // N_KV_HEADS] = part
        else:
            blk_scr[c // N_KV_HEADS] += part

    def in_finish(c):
        t = blk_scr[c // N_KV_HEADS].T
        if c < N_KV_HEADS:
            for j in range(Q_PER_KV):
                lo = c * HEAD_TILE + j * HEAD_DIM
                x1, x2 = t[j * HEAD_DIM:j * HEAD_DIM + half], t[j * HEAD_DIM + half:(j + 1) * HEAD_DIM]
                qt_scr[lo:lo + half] = ((x1 * cos_t - x2 * sin_t) * qk_scale).astype(_BF)
                qt_scr[lo + half:lo + HEAD_DIM] = ((x2 * cos_t + x1 * sin_t) * qk_scale).astype(_BF)
        else:
            lo = (c - N_KV_HEADS) * HEAD_TILE
            gt_scr[lo:lo + HEAD_TILE] = jax.nn.silu(t)

    def out_prepare(kv):
        ob_scr[...] = ot_scr[kv * HEAD_TILE:(kv + 1) * HEAD_TILE].T.astype(_BF)

    def out_piece(kv, n):
        part = _dot(ob_scr[...], wout_ref[kv * HEAD_TILE:(kv + 1) * HEAD_TILE, n * HEAD_TILE:(n + 1) * HEAD_TILE])
        if kv == 0:
            acc_scr[:, n * HEAD_TILE:(n + 1) * HEAD_TILE] = part
        else:
            acc_scr[:, n * HEAD_TILE:(n + 1) * HEAD_TILE] += part

    key_blk = lax.broadcasted_iota(jnp.int32, (PAIR_KEYS, LANES), 0) // CHUNK
    qry_blk = lax.broadcasted_iota(jnp.int32, (PAIR_KEYS, LANES), 1) // CHUNK
    no_q = jnp.zeros((HEAD_DIM, LANES), _BF)
    pairs = []
    for a in range(n_pairs):
        if banded:
            qc0 = pl.program_id(1) * (rb // CHUNK) + 2 * (a % pairs_per_stream)
            kc0 = jnp.maximum(qc0 - WINDOW_CHUNKS, 0)
            delta0 = kc0 - qc0
            kstart = pl.multiple_of(kc0 * CHUNK, LANES)
            vblk = kc0 // 2
        else:
            delta0, kstart, vblk = -WINDOW_CHUNKS, 0, 0
        rel = delta0 + key_blk - qry_blk
        cap_scr[a] = jnp.where((rel >= -WINDOW_CHUNKS) & (rel <= 0), MASK_PASS, NEG_INF)
        pairs.append((a // pairs_per_stream, kstart, vblk))

    def unit_heads(kv, gp):
        return (kv * Q_PER_KV + 2 * gp, kv * Q_PER_KV + 2 * gp + 1)

    def scores(a, kv, gp):
        lanes = slice(a * LANES, (a + 1) * LANES)
        stream, kstart, _ = pairs[a]
        k_tile = k_ref[stream, pl.ds(kstart, PAIR_KEYS), (kv // 2) * LANES:(kv // 2 + 1) * LANES]
        w_parts = []
        for h in unit_heads(kv, gp):
            q_h = qt_scr[h * HEAD_DIM:(h + 1) * HEAD_DIM, lanes]
            w_parts.append(jnp.concatenate([q_h, no_q] if kv % 2 == 0 else [no_q, q_h], axis=0))
        return _dot(k_tile, jnp.concatenate(w_parts, axis=1))

    def softmax_values(a, kv, gp, s):
        e_parts, inv_den = [], []
        for u, h in enumerate(unit_heads(kv, gp)):
            s_h = jnp.minimum(s[:, u * LANES:(u + 1) * LANES], cap_scr[a])
            sink = sinks_ref[h]
            m = jnp.maximum(_reduce_rows(s_h, jnp.maximum, jnp.max), sink)
            e = jnp.exp(s_h - m)
            inv_den.append(1.0 / (_reduce_rows(e, jnp.add, jnp.sum) + jnp.exp(sink - m)))
            e_parts.append(e.astype(_BF))
        stream, _, vblk = pairs[a]
        vt_kv = jnp.concatenate([vt_ref[stream, vblk, kv * HEAD_DIM:(kv + 1) * HEAD_DIM, :],
                                 vt_ref[stream, vblk + 1, kv * HEAD_DIM:(kv + 1) * HEAD_DIM, :]], axis=1)
        return _dot(vt_kv, jnp.concatenate(e_parts, axis=1)), inv_den

    def finish(a, kv, gp, ot, inv_den):
        lanes = slice(a * LANES, (a + 1) * LANES)
        for u, h in enumerate(unit_heads(kv, gp)):
            rows_h = slice(h * HEAD_DIM, (h + 1) * HEAD_DIM)
            ot_scr[rows_h, lanes] = ot[:, u * LANES:(u + 1) * LANES] * inv_den[u] * gt_scr[rows_h, lanes]

    n_k = D_MODEL // kt
    for c in (0, N_KV_HEADS):
        for k in range(n_k):
            in_piece(c, k)
        in_finish(c)
    slots = []
    for kv in range(N_KV_HEADS):
        units = [(a, kv, gp) for a in range(n_pairs) for gp in range(Q_PER_KV // 2)]
        ahead, behind = [], []
        if kv + 1 < N_KV_HEADS:
            for c in (kv + 1, kv + 1 + N_KV_HEADS):
                ahead += [functools.partial(in_piece, c, k) for k in range(n_k)] + [functools.partial(in_finish, c)]
        if kv > 0:
            behind = [functools.partial(out_prepare, kv - 1)] + [
                functools.partial(out_piece, kv - 1, n) for n in range(D_MODEL // HEAD_TILE)]
        work = [[] for _ in units]
        for m, item in enumerate(ahead):
            work[m * len(units) // len(ahead)].append(item)
        for m, item in enumerate(behind):
            work[1 + m * (len(units) - 1) // len(behind)].append(item)
        slots += list(zip(units, work))
    s_next, unfinished = scores(*slots[0][0]), None
    for i, (unit, work) in enumerate(slots):
        for item in work:
            item()
        s_cur = s_next
        if i + 1 < len(slots):
            s_next = scores(*slots[i + 1][0])
        ot, inv_den = softmax_values(*unit, s_cur)
        if unfinished is not None:
            finish(*unfinished)
        unfinished = (*unit, ot, inv_den)
    finish(*unfinished)
    out_prepare(N_KV_HEADS - 1)
    for n in range(D_MODEL // HEAD_TILE):
        out_piece(N_KV_HEADS - 1, n)

    r = DEEPNORM_ALPHA * x + gate_ref[...] * acc_scr[...].reshape(bb, rb, D_MODEL)
    y_ref[...] = _layer_norm(r, lng_ref[...], lnb_ref[...])


def _attn_layer(x, ada, win, wout, sinks, tables_t, k_bf, vt, lng, lnb, banded, bb, rb):
    bsz, seq, _ = x.shape
    rows = bb * rb
    keys = k_bf.shape[1]
    shift, scale, gate = (a.reshape(bsz, 1, D_MODEL) for a in ada)
    vec = lambda a: a.reshape(1, D_MODEL)
    row_spec = pl.BlockSpec((bb, rb, D_MODEL), lambda b, i: (b, i, 0))
    ada_spec = pl.BlockSpec((bb, 1, D_MODEL), lambda b, i: (b, 0, 0))
    tab_spec = pl.BlockSpec((HEAD_DIM // 2, rows), lambda b, i: (0, i))
    kern = functools.partial(_attn_layer_kernel, bb=bb, rb=rb, banded=banded)
    return pl.pallas_call(
        kern,
        grid=(bsz // bb, seq // rb),
        in_specs=[pl.BlockSpec(memory_space=pltpu.SMEM),
                  row_spec, ada_spec, ada_spec, ada_spec,
                  _const_spec((D_MODEL, 2 * D_MODEL)), _const_spec((D_MODEL, D_MODEL)),
                  tab_spec, tab_spec,
                  pl.BlockSpec((bb, keys, KV_WIDTH), lambda b, i: (b, 0, 0)),
                  pl.BlockSpec((bb, keys // LANES, KV_WIDTH, LANES), lambda b, i: (b, 0, 0, 0)),
                  _const_spec((1, D_MODEL)), _const_spec((1, D_MODEL))],
        out_specs=row_spec,
        out_shape=jax.ShapeDtypeStruct((bsz, seq, D_MODEL), _F32),
        scratch_shapes=[pltpu.VMEM((rows, D_MODEL), _BF),
                        pltpu.VMEM((2, rows, HEAD_TILE), _F32),
                        pltpu.VMEM((D_MODEL, rows), _BF),
                        pltpu.VMEM((D_MODEL, rows), _F32),
                        pltpu.VMEM((D_MODEL, rows), _F32),
                        pltpu.VMEM((rows, HEAD_TILE), _BF),
                        pltpu.VMEM((rows, D_MODEL), _F32),
                        pltpu.VMEM((rows // LANES, PAIR_KEYS, LANES), _F32)],
        compiler_params=pltpu.CompilerParams(dimension_semantics=("arbitrary", "arbitrary"),
                                             vmem_limit_bytes=VMEM_LIMIT),
        name="attn_layer",
    )(sinks, x, shift, scale, gate, win, wout, *tables_t, k_bf, vt, vec(lng), vec(lnb))


def _run_trunk(x, ada_all, pos, h0, cache_k, cache_v, p):
    bsz, seq, _ = x.shape
    ada = lambda l: tuple(ada_all[l][:, k * D_MODEL:(k + 1) * D_MODEL] for k in range(3))
    states = []
    for la in range(N_A_LAYERS):
        x, h_last = _s5_layer(x, bsz, seq, ada(la), h0[la], p['win_a'][la], p['bs'][la], p['cs'][la],
                              p['ar'][la], p['ai'][la], p['ssm_d'][la], p['w_glu'][la], p['b_glu'][la],
                              p['w_out_a'][la], p['ln_g'][la], p['ln_b'][la],
                              batch_major_in=la == 0, batch_major_out=la == N_A_LAYERS - 1)
        states.append(_strips_to_state(h_last))
    banded = cache_k is None
    outs = _shared_kv(x, p['w_kv'], _rope_tables(pos), transposed_v=banded)
    k_new, v_new, k_bf = outs[:3]
    if banded:
        vt = outs[3]
        q_pos = pos
    else:
        pad = LANES - seq
        flat = lambda c: c.reshape(bsz, c.shape[1], KV_WIDTH)
        k_bf = jnp.concatenate([flat(cache_k).astype(_BF), k_bf, jnp.zeros((bsz, pad, KV_WIDTH), _BF)], axis=1)
        v_all = jnp.concatenate([flat(cache_v), v_new, jnp.zeros((bsz, pad, KV_WIDTH), _F32)], axis=1)
        vt = v_all.reshape(bsz, PAIR_KEYS // LANES, LANES, KV_WIDTH).transpose(0, 1, 3, 2).astype(_BF)
        x = jnp.pad(x, ((0, 0), (0, pad), (0, 0)))
        q_pos = jnp.tile(pos[0] + jnp.arange(LANES, dtype=jnp.int32), bsz)
    cos, sin = _rope_angles(q_pos)
    tables_t = (cos.T, sin.T)
    bb, rb = (1, ATTN_ROWS) if banded else (bsz, LANES)
    for lb in range(N_B_LAYERS):
        layer = N_A_LAYERS + lb
        x = _attn_layer(x, ada(layer), p['win_b'][lb], p['w_out_b'][lb], p['attn_sinks'][lb], tables_t, k_bf, vt,
                        p['ln_g'][layer], p['ln_b'][layer], banded=banded, bb=bb, rb=rb)
    k4 = k_new.reshape(bsz, seq, N_KV_HEADS, HEAD_DIM)
    v4 = v_new.reshape(bsz, seq, N_KV_HEADS, HEAD_DIM)
    return x[:, :seq], jnp.stack(states), k4, v4


def kernel(x_prompt, x_sample, state_ssm, cache_k, cache_v, c_prompt, c_sample, w_ada, b_ada, ln_g, ln_b, w_in_a,
           ssm_a_re, ssm_a_im, ssm_b_re, ssm_b_im, ssm_c_re, ssm_c_im, ssm_d, ssm_log_dt, w_glu, b_glu, w_out_a,
           w_kv, w_in_b, attn_sinks, w_out_b):
    n_prompt, n_sample = x_prompt.shape[0], x_sample.shape[0]
    ada_all = _ada_params(jnp.concatenate([c_prompt, c_sample], axis=0), w_ada, b_ada)
    ab_re, ab_im, bb_re, bb_im = _s5_prep(ssm_a_re, ssm_a_im, ssm_log_dt, ssm_b_re, ssm_b_im)
    bs, cs, ar, ai = _s5_matrices(ab_re, ab_im, bb_re, bb_im, ssm_c_re, ssm_c_im)
    p = dict(win_a=w_in_a.astype(_BF), bs=bs, cs=cs, ar=ar, ai=ai, ssm_d=ssm_d, w_glu=w_glu.astype(_BF),
             b_glu=b_glu, w_out_a=w_out_a.astype(_BF), ln_g=ln_g, ln_b=ln_b, w_kv=w_kv.astype(_BF),
             win_b=w_in_b.astype(_BF), w_out_b=w_out_b.astype(_BF), attn_sinks=attn_sinks)

    pos_prompt = jnp.arange(x_prompt.shape[1], dtype=jnp.int32)
    pos_sample = PAST_LEN + jnp.arange(x_sample.shape[1], dtype=jnp.int32)
    h0_prompt = jnp.zeros((N_A_LAYERS, N_STRIPS, n_prompt, STRIP_W), _F32)
    h0_sample = jnp.stack([_state_to_strips(state_ssm[la]) for la in range(N_A_LAYERS)])

    y_p, ssm_p, k_p, v_p = _run_trunk(x_prompt, ada_all[:, :n_prompt], pos_prompt, h0_prompt, None, None, p)
    y_s, ssm_s, k_s, v_s = _run_trunk(x_sample, ada_all[:, n_prompt:], pos_sample, h0_sample, cache_k, cache_v, p)
    rows = min(WINDOW, x_prompt.shape[1])
    return (y_p, y_s, ssm_p, k_p[:, -rows:], v_p[:, -rows:], ssm_s, k_s, v_s)
```

```python
import functools
import math

import jax
import jax.numpy as jnp
from jax import lax
from jax.experimental import pallas as pl
from jax.experimental.pallas import tpu as pltpu

D_MODEL = 1024
DEPTH = 4
CHUNK = 64
N_A_LAYERS = 2
N_B_LAYERS = 2
SSM_GROUP_CH = 16
SSM_GROUPS = 64
SSM_STATE = 64
HEAD_DIM = 64
N_HEADS = 16
N_KV_HEADS = 4
Q_PER_KV = 4
KV_WIDTH = N_KV_HEADS * HEAD_DIM
WINDOW = 128
WINDOW_CHUNKS = WINDOW // CHUNK
PAST_LEN = 1024
ROPE_THETA = 10000.0
NEG_INF = -1e30
DEEPNORM_ALPHA = (2.0 * DEPTH) ** 0.25
LN_EPS = 1e-5

LANES = 128
STRIP_GROUPS = LANES // SSM_GROUP_CH
N_STRIPS = SSM_GROUPS // STRIP_GROUPS
STRIP_HALF = STRIP_GROUPS * SSM_STATE
STRIP_W = 2 * STRIP_HALF
SCAN_W = 256
S5_ROWS = 512
KEYS = (WINDOW_CHUNKS + 1) * CHUNK
PAIR_KEYS = KEYS + CHUNK
ATTN_ROWS = 512
HEAD_TILE = Q_PER_KV * HEAD_DIM
IN_PIECE = 256
KV_ROWS = 512
MASK_PASS = 3.0e38
VMEM_LIMIT = 56 * 1024 * 1024

_BF = jnp.bfloat16
_F32 = jnp.float32


def _dot(a, b):
    return jnp.dot(a, b, preferred_element_type=_F32)


def _const_spec(shape):
    nd = len(shape)
    return pl.BlockSpec(shape, lambda *_: (0,) * nd, pipeline_mode=pl.Buffered(1))


def _layer_norm(r, g, b):
    mu = jnp.mean(r, axis=-1, keepdims=True)
    d = r - mu
    var = jnp.mean(d * d, axis=-1, keepdims=True)
    return d * lax.rsqrt(var + LN_EPS) * g + b


def _ada_kernel(c_ref, w_ref, b_ref, o_ref):
    c = c_ref[...]
    o_ref[0] = _dot(jax.nn.silu(c).astype(_BF), w_ref[0].astype(_BF)) + b_ref[0]


def _ada_params(c_all, w_ada, b_ada):
    n = c_all.shape[0]
    return pl.pallas_call(
        _ada_kernel,
        grid=(DEPTH, 3),
        in_specs=[pl.BlockSpec((n, D_MODEL), lambda l, j: (0, 0)),
                  pl.BlockSpec((1, D_MODEL, D_MODEL), lambda l, j: (l, 0, j)),
                  pl.BlockSpec((1, 1, D_MODEL), lambda l, j: (l, 0, j))],
        out_specs=pl.BlockSpec((1, n, D_MODEL), lambda l, j: (l, 0, j)),
        out_shape=jax.ShapeDtypeStruct((DEPTH, n, 3 * D_MODEL), _F32),
        name="ada_params",
    )(c_all, w_ada, b_ada.reshape(DEPTH, 1, 3 * D_MODEL))


def _cmul(x_re, x_im, y_re, y_im):
    return x_re * y_re - x_im * y_im, x_re * y_im + x_im * y_re


def _zoh_a(a_re, a_im, log_dt):
    dt = jnp.exp(log_dt)
    mag = jnp.exp(a_re * dt)
    return mag * jnp.cos(a_im * dt), mag * jnp.sin(a_im * dt)


def _s5_prep_kernel(are_ref, aim_ref, ldt_ref, bre_ref, bim_ref, aret_ref, aimt_ref, ldtt_ref, cre_ref, cim_ref,
                    a2r_ref, a2i_ref, bbr_ref, bbi_ref, abr_ref, abi_ref, car_ref, cai_ref, ca2r_ref, ca2i_ref):
    a_re, a_im = are_ref[0], aim_ref[0]
    ab_re, ab_im = _zoh_a(a_re, a_im, ldt_ref[0])
    nr, ni = ab_re - 1.0, ab_im
    den = a_re * a_re + a_im * a_im
    f_re = (nr * a_re + ni * a_im) / den
    f_im = (ni * a_re - nr * a_im) / den
    bb_re, bb_im = _cmul(f_re, f_im, bre_ref[0], bim_ref[0])
    a2r_ref[0], a2i_ref[0] = _cmul(ab_re, ab_im, ab_re, ab_im)
    bbr_ref[0], bbi_ref[0] = bb_re, bb_im
    abr_ref[0], abi_ref[0] = _cmul(ab_re, ab_im, bb_re, bb_im)
    at_re, at_im = _zoh_a(aret_ref[0], aimt_ref[0], ldtt_ref[0])
    ca_re, ca_im = _cmul(cre_ref[0], cim_ref[0], at_re, at_im)
    car_ref[0], cai_ref[0] = ca_re, ca_im
    ca2r_ref[0], ca2i_ref[0] = _cmul(ca_re, ca_im, at_re, at_im)


def _s5_lag_kernel(cr_ref, ci_ref, car_ref, cai_ref, bbr_ref, bbi_ref, k0_ref, k1_ref):
    def re_prod(x_re, x_im):
        dot = lambda x, y: jnp.einsum('gcp,gpd->gcd', x, y, precision=lax.Precision.HIGHEST,
                                      preferred_element_type=_F32)
        return dot(x_re, bbr_ref[0]) - dot(x_im, bbi_ref[0])
    k0_ref[0] = re_prod(cr_ref[0], ci_ref[0])
    k1_ref[0] = re_prod(car_ref[0], cai_ref[0])


def _s5_prep(a_re, a_im, log_dt, b_re, b_im, c_re, c_im):
    nl, w = N_A_LAYERS, SSM_STATE * SSM_GROUP_CH
    flat = lambda a: a.reshape(nl, SSM_GROUPS, w)
    rep = lambda a: jnp.repeat(a, SSM_GROUP_CH, axis=-1)
    til = lambda a: jnp.tile(a, (1, 1, SSM_GROUP_CH))
    ldt = log_dt[:, :, None]
    spec = pl.BlockSpec((1, SSM_GROUPS, w), lambda l: (l, 0, 0))
    shp = jax.ShapeDtypeStruct((nl, SSM_GROUPS, w), _F32)
    a2_re, a2_im, bb_re, bb_im, abb_re, abb_im, ca_re, ca_im, ca2_re, ca2_im = pl.pallas_call(
        _s5_prep_kernel, grid=(nl,), in_specs=[spec] * 10, out_specs=[spec] * 10, out_shape=[shp] * 10, name="s5_prep",
    )(rep(a_re), rep(a_im), rep(jnp.broadcast_to(ldt, a_re.shape)), flat(b_re), flat(b_im),
      til(a_re), til(a_im), til(jnp.broadcast_to(ldt, a_re.shape)), flat(c_re), flat(c_im))
    pc = (nl, SSM_GROUPS, SSM_STATE, SSM_GROUP_CH)
    cp = (nl, SSM_GROUPS, SSM_GROUP_CH, SSM_STATE)
    cp_spec = pl.BlockSpec((1,) + cp[1:], lambda l: (l, 0, 0, 0))
    pc_spec = pl.BlockSpec((1,) + pc[1:], lambda l: (l, 0, 0, 0))
    kk = (nl, SSM_GROUPS, SSM_GROUP_CH, SSM_GROUP_CH)
    kk_spec = pl.BlockSpec((1,) + kk[1:], lambda l: (l, 0, 0, 0))
    k0, k1 = pl.pallas_call(
        _s5_lag_kernel, grid=(nl,), in_specs=[cp_spec] * 4 + [pc_spec] * 2, out_specs=[kk_spec] * 2,
        out_shape=[jax.ShapeDtypeStruct(kk, _F32)] * 2, name="s5_lag",
    )(c_re, c_im, ca_re.reshape(cp), ca_im.reshape(cp), bb_re.reshape(pc), bb_im.reshape(pc))
    return dict(a2_re=a2_re.reshape(pc)[..., 0], a2_im=a2_im.reshape(pc)[..., 0],
                bb_re=bb_re.reshape(pc), bb_im=bb_im.reshape(pc), abb_re=abb_re.reshape(pc), abb_im=abb_im.reshape(pc),
                ca_re=ca_re.reshape(cp), ca_im=ca_im.reshape(cp), ca2_re=ca2_re.reshape(cp), ca2_im=ca2_im.reshape(cp),
                k0=k0, k1=k1)


def _s5_matrices(q):
    eye = jnp.eye(STRIP_GROUPS, dtype=_F32)
    nl = N_A_LAYERS

    def expand_b(re, im):
        bb = jnp.stack([re, im], axis=1).reshape(nl, 2, N_STRIPS, STRIP_GROUPS, SSM_STATE, SSM_GROUP_CH)
        return jnp.einsum('lrjgpc,gh->ljgcrhp', bb, eye).reshape(nl, N_STRIPS, LANES, STRIP_W)

    def project_c(re, im):
        cc = jnp.stack([re, -im], axis=1).reshape(nl, 2, N_STRIPS, STRIP_GROUPS, SSM_GROUP_CH, SSM_STATE)
        return jnp.einsum('lrjgcp,gh->ljrhpgc', cc, eye).reshape(nl, N_STRIPS, STRIP_W, LANES)

    def lag(k):
        kt = k.transpose(0, 1, 3, 2).reshape(nl, N_STRIPS, STRIP_GROUPS, SSM_GROUP_CH, SSM_GROUP_CH)
        return jnp.einsum('ljgdc,gh->ljgdhc', kt, eye).reshape(nl, N_STRIPS, LANES, LANES)

    bs2 = jnp.concatenate([expand_b(q['abb_re'], q['abb_im']), expand_b(q['bb_re'], q['bb_im'])], axis=2)
    cs2 = jnp.concatenate([project_c(q['ca_re'], q['ca_im']), project_c(q['ca2_re'], q['ca2_im'])], axis=3)
    k0, k1 = lag(q['k0']), lag(q['k1'])
    ks = jnp.concatenate([jnp.concatenate([k0, k1], axis=3),
                          jnp.concatenate([jnp.zeros_like(k0), k0], axis=3)], axis=2)
    a2r = q['a2_re'].reshape(nl, N_STRIPS, 1, STRIP_HALF)
    a2i = q['a2_im'].reshape(nl, N_STRIPS, 1, STRIP_HALF)
    return bs2.astype(_BF), cs2.astype(_BF), ks.astype(_BF), a2r, a2i


def _state_to_strips(state):
    nb = state.shape[0]
    s = state.reshape(nb, N_STRIPS, STRIP_GROUPS, SSM_STATE, 2)
    return s.transpose(1, 0, 4, 2, 3).reshape(N_STRIPS, nb, STRIP_W)


def _strips_to_state(h):
    nb = h.shape[1]
    s = h.reshape(N_STRIPS, nb, 2, STRIP_GROUPS, SSM_STATE)
    return s.transpose(1, 0, 3, 4, 2).reshape(nb, SSM_GROUPS, SSM_STATE, 2)


def _s5_layer_kernel(x_ref, shift_ref, scale_ref, gate_ref, h0_ref, win_ref, bs_ref, cs_ref, ks_ref, ar_ref, ai_ref,
                     d_ref, wglu_ref, bglu_ref, wout_ref, lng_ref, lnb_ref,
                     y_ref, hout_ref,
                     h_scr, u_scr, z_scr, bu_scr, yy_scr, xt_scr, *, nb, steps, batch_major_in, batch_major_out):
    rows = nb * steps
    i = pl.program_id(0)

    @pl.when(i == 0)
    def _():
        h_scr[...] = h0_ref[...]

    lane_tiles = D_MODEL // LANES
    if batch_major_in:
        for b in range(nb):
            for c in range(lane_tiles):
                xt_scr[c, pl.ds(b, steps, stride=nb), :] = x_ref[b, :, c * LANES:(c + 1) * LANES]
        x_rows = jnp.concatenate([xt_scr[c] for c in range(lane_tiles)], axis=1)
    else:
        x_rows = x_ref[...]
    x3 = x_rows.reshape(steps, nb, D_MODEL)
    hmod = (x3 * (1.0 + scale_ref[...])[None] + shift_ref[...][None]).reshape(rows, D_MODEL)
    uz = _dot(hmod.astype(_BF), win_ref[...])
    pairs = steps // 2
    prow = pairs * nb
    for j in range(N_STRIPS):
        u_scr[j] = uz[:, j * LANES:(j + 1) * LANES].reshape(pairs, 2, nb, LANES)
    z_scr[...] = uz[:, D_MODEL:]

    def pair_inputs(j):
        return jnp.concatenate([u_scr[j, :, 0].reshape(prow, LANES), u_scr[j, :, 1].reshape(prow, LANES)], axis=1)

    def expand(j):
        bu_scr[j % 2] = _dot(pair_inputs(j).astype(_BF), bs_ref[j])

    def scan(j):
        buf = j % 2
        a_re_row, a_im_row = ar_ref[j], ai_ref[j]
        for s in range(STRIP_HALF // SCAN_W):
            re = slice(s * SCAN_W, (s + 1) * SCAN_W)
            im = slice(STRIP_HALF + s * SCAN_W, STRIP_HALF + (s + 1) * SCAN_W)
            a_re = jnp.broadcast_to(a_re_row[:, re], (nb, SCAN_W))
            a_im = jnp.broadcast_to(a_im_row[:, re], (nb, SCAN_W))
            h_re, h_im = h_scr[j, :, re], h_scr[j, :, im]
            for t in range(pairs):
                now = slice(t * nb, (t + 1) * nb)
                n_re = a_re * h_re - a_im * h_im + bu_scr[buf, now, re]
                n_im = a_re * h_im + a_im * h_re + bu_scr[buf, now, im]
                bu_scr[buf, now, re] = h_re
                bu_scr[buf, now, im] = h_im
                h_re, h_im = n_re, n_im
            h_scr[j, :, re] = h_re
            h_scr[j, :, im] = h_im

    def project(j):
        yy = _dot(bu_scr[j % 2].astype(_BF), cs_ref[j]) + _dot(pair_inputs(j).astype(_BF), ks_ref[j])
        for par in range(2):
            y_par = yy[:, par * LANES:(par + 1) * LANES].reshape(pairs, nb, LANES) + d_ref[j] * u_scr[j, :, par]
            yy_scr[j, :, par] = y_par

    expand(0)
    for j in range(N_STRIPS):
        if j + 1 < N_STRIPS:
            expand(j + 1)
        scan(j)
        project(j)

    y = jnp.concatenate([yy_scr[j].reshape(rows, LANES) for j in range(N_STRIPS)], axis=1)
    g = jax.nn.gelu(y)
    y = g * jax.nn.sigmoid(_dot(g.astype(_BF), wglu_ref[...]) + bglu_ref[...])
    y = y * jax.nn.silu(z_scr[...])
    out = _dot(y.astype(_BF), wout_ref[...]).reshape(steps, nb, D_MODEL)
    r = (DEEPNORM_ALPHA * x3 + gate_ref[...][None] * out).reshape(rows, D_MODEL)
    y = _layer_norm(r, lng_ref[...], lnb_ref[...])
    if batch_major_out:
        for c in range(lane_tiles):
            xt_scr[c] = y[:, c * LANES:(c + 1) * LANES]
        for b in range(nb):
            for c in range(lane_tiles):
                y_ref[b, :, c * LANES:(c + 1) * LANES] = xt_scr[c, pl.ds(b, steps, stride=nb), :]
    else:
        y_ref[...] = y

    @pl.when(i == pl.num_programs(0) - 1)
    def _():
        hout_ref[...] = h_scr[...]


def _s5_layer(x, nb, seq, ada, h0, win, bs, cs, ks, ar, ai, d, wglu, bglu, wout, lng, lnb, batch_major_in,
              batch_major_out):
    total = seq * nb
    rows = min(S5_ROWS, total)
    steps = rows // nb
    shift, scale, gate = ada
    tb_spec = pl.BlockSpec((rows, D_MODEL), lambda i: (i, 0))
    bt_spec = pl.BlockSpec((nb, steps, D_MODEL), lambda i: (0, i, 0))
    tb_shape = jax.ShapeDtypeStruct((total, D_MODEL), _F32)
    bt_shape = jax.ShapeDtypeStruct((nb, seq, D_MODEL), _F32)
    vec = lambda a: a.reshape(1, D_MODEL)
    kern = functools.partial(_s5_layer_kernel, nb=nb, steps=steps, batch_major_in=batch_major_in,
                             batch_major_out=batch_major_out)
    return pl.pallas_call(
        kern,
        grid=(total // rows,),
        in_specs=[bt_spec if batch_major_in else tb_spec,
                  _const_spec((nb, D_MODEL)), _const_spec((nb, D_MODEL)), _const_spec((nb, D_MODEL)),
                  _const_spec((N_STRIPS, nb, STRIP_W)),
                  _const_spec((D_MODEL, 2 * D_MODEL)),
                  _const_spec((N_STRIPS, 2 * LANES, STRIP_W)),
                  _const_spec((N_STRIPS, STRIP_W, 2 * LANES)),
                  _const_spec((N_STRIPS, 2 * LANES, 2 * LANES)),
                  _const_spec((N_STRIPS, 1, STRIP_HALF)), _const_spec((N_STRIPS, 1, STRIP_HALF)),
                  _const_spec((N_STRIPS, 1, LANES)),
                  _const_spec((D_MODEL, D_MODEL)), _const_spec((1, D_MODEL)),
                  _const_spec((D_MODEL, D_MODEL)),
                  _const_spec((1, D_MODEL)), _const_spec((1, D_MODEL))],
        out_specs=[bt_spec if batch_major_out else tb_spec,
                   pl.BlockSpec((N_STRIPS, nb, STRIP_W), lambda i: (0, 0, 0))],
        out_shape=[bt_shape if batch_major_out else tb_shape,
                   jax.ShapeDtypeStruct((N_STRIPS, nb, STRIP_W), _F32)],
        scratch_shapes=[pltpu.VMEM((N_STRIPS, nb, STRIP_W), _F32),
                        pltpu.VMEM((N_STRIPS, steps // 2, 2, nb, LANES), _F32),
                        pltpu.VMEM((rows, D_MODEL), _F32),
                        pltpu.VMEM((2, rows // 2, STRIP_W), _F32),
                        pltpu.VMEM((N_STRIPS, steps // 2, 2, nb, LANES), _F32),
                        pltpu.VMEM((D_MODEL // LANES, rows, LANES), _F32)],
        compiler_params=pltpu.CompilerParams(dimension_semantics=("arbitrary",), vmem_limit_bytes=VMEM_LIMIT),
        name="s5_layer",
    )(x, shift, scale, gate, h0, win, bs, cs, ks, ar, ai, d.reshape(N_STRIPS, 1, LANES),
      wglu, vec(bglu), wout, vec(lng), vec(lnb))


def _rope_angles(pos):
    half = HEAD_DIM // 2
    inv_freq = jnp.power(ROPE_THETA, -jnp.arange(half, dtype=_F32) / half)
    ang = pos.astype(_F32)[:, None] * inv_freq[None, :]
    return jnp.cos(ang), jnp.sin(ang)


def _rope_tables(pos):
    cos, sin = _rope_angles(pos)
    zero = jnp.zeros_like(sin)
    cos_t = jnp.tile(cos, (1, LANES // (HEAD_DIM // 2)))
    sin_lo = jnp.tile(jnp.concatenate([-sin, zero], axis=1), (1, LANES // HEAD_DIM))
    sin_hi = jnp.tile(jnp.concatenate([zero, sin], axis=1), (1, LANES // HEAD_DIM))
    return cos_t, sin_lo, sin_hi


def _rope_tile(x, cos_t, sin_lo, sin_hi):
    half = HEAD_DIM // 2
    return x * cos_t + pltpu.roll(x, LANES - half, 1) * sin_lo + pltpu.roll(x, half, 1) * sin_hi


def _kv_kernel(x_ref, w_ref, cos_ref, slo_ref, shi_ref, k_ref, v_ref, kbf_ref, *vt_ref):
    kv = _dot(x_ref[0].astype(_BF), w_ref[...])
    cos_t, sin_lo, sin_hi = cos_ref[...], slo_ref[...], shi_ref[...]
    for t in range(KV_WIDTH // LANES):
        tile = _rope_tile(kv[:, t * LANES:(t + 1) * LANES], cos_t, sin_lo, sin_hi)
        k_ref[0, :, t * LANES:(t + 1) * LANES] = tile
        kbf_ref[0, :, t * LANES:(t + 1) * LANES] = tile.astype(_BF)
    v = kv[:, KV_WIDTH:]
    v_ref[0] = v
    if vt_ref:
        for blk in range(v.shape[0] // LANES):
            vt_ref[0][0, blk] = v[blk * LANES:(blk + 1) * LANES, :].T.astype(_BF)


def _shared_kv(x, w_kv, tables, transposed_v):
    bsz, seq, _ = x.shape
    rows = min(KV_ROWS, seq)
    tab_spec = pl.BlockSpec((rows, LANES), lambda b, i: (i, 0))
    kv_spec = pl.BlockSpec((1, rows, KV_WIDTH), lambda b, i: (b, i, 0))
    out_specs = [kv_spec] * 3
    out_shape = ([jax.ShapeDtypeStruct((bsz, seq, KV_WIDTH), _F32)] * 2
                 + [jax.ShapeDtypeStruct((bsz, seq, KV_WIDTH), _BF)])
    if transposed_v:
        out_specs.append(pl.BlockSpec((1, rows // LANES, KV_WIDTH, LANES), lambda b, i: (b, i, 0, 0)))
        out_shape.append(jax.ShapeDtypeStruct((bsz, seq // LANES, KV_WIDTH, LANES), _BF))
    return pl.pallas_call(
        _kv_kernel,
        grid=(bsz, seq // rows),
        in_specs=[pl.BlockSpec((1, rows, D_MODEL), lambda b, i: (b, i, 0)),
                  _const_spec((D_MODEL, 2 * KV_WIDTH)), tab_spec, tab_spec, tab_spec],
        out_specs=out_specs, out_shape=out_shape,
        compiler_params=pltpu.CompilerParams(dimension_semantics=("arbitrary", "arbitrary"),
                                             vmem_limit_bytes=VMEM_LIMIT),
        name="shared_kv",
    )(x, w_kv, *tables)


def _reduce_rows(x, pair_op, final_op):
    sub = 8
    parts = [x[i * sub:(i + 1) * sub] for i in range(x.shape[0] // sub)]
    while len(parts) > 1:
        parts = [pair_op(parts[i], parts[i + 1]) for i in range(0, len(parts), 2)]
    return final_op(parts[0], axis=0, keepdims=True)


def _attn_layer_kernel(sinks_ref, x_ref, shift_ref, scale_ref, gate_ref, win_ref, wout_ref, cos_ref, sin_ref,
                       k_ref, vt_ref, lng_ref, lnb_ref, y_ref,
                       hm_scr, blk_scr, qt_scr, gt_scr, ot_scr, ob_scr, acc_scr, cap_scr, *, bb, rb, banded):
    rows = bb * rb
    n_pairs = rows // LANES
    pairs_per_stream = rb // LANES
    x = x_ref[...]
    hm_scr[...] = (x * (1.0 + scale_ref[...]) + shift_ref[...]).reshape(rows, D_MODEL).astype(_BF)
    cos_t, sin_t = cos_ref[...], sin_ref[...]
    half = HEAD_DIM // 2
    qk_scale = HEAD_DIM ** -0.5
    kt = IN_PIECE

    def in_piece(c, k):
        part = _dot(hm_scr[:, k * kt:(k + 1) * kt], win_ref[k * kt:(k + 1) * kt, c * HEAD_TILE:(c + 1) * HEAD_TILE])
        if k == 0:
            blk_scr[c // N_KV_HEADS] = part
        else:
            blk_scr[c // N_KV_HEADS] += part

    def in_finish(c):
        t = blk_scr[c // N_KV_HEADS].T
        if c < N_KV_HEADS:
            for j in range(Q_PER_KV):
                lo = c * HEAD_TILE + j * HEAD_DIM
                x1, x2 = t[j * HEAD_DIM:j * HEAD_DIM + half], t[j * HEAD_DIM + half:(j + 1) * HEAD_DIM]
                qt_scr[lo:lo + half] = ((x1 * cos_t - x2 * sin_t) * qk_scale).astype(_BF)
                qt_scr[lo + half:lo + HEAD_DIM] = ((x2 * cos_t + x1 * sin_t) * qk_scale).astype(_BF)
        else:
            lo = (c - N_KV_HEADS) * HEAD_TILE
            gt_scr[lo:lo + HEAD_TILE] = jax.nn.silu(t)

    def out_prepare(kv):
        ob_scr[...] = ot_scr[kv * HEAD_TILE:(kv + 1) * HEAD_TILE].T.astype(_BF)

    def out_piece(kv, n):
        part = _dot(ob_scr[...], wout_ref[kv * HEAD_TILE:(kv + 1) * HEAD_TILE, n * HEAD_TILE:(n + 1) * HEAD_TILE])
        if kv == 0:
            acc_scr[:, n * HEAD_TILE:(n + 1) * HEAD_TILE] = part
        else:
            acc_scr[:, n * HEAD_TILE:(n + 1) * HEAD_TILE] += part

    key_blk = lax.broadcasted_iota(jnp.int32, (PAIR_KEYS, LANES), 0) // CHUNK
    qry_blk = lax.broadcasted_iota(jnp.int32, (PAIR_KEYS, LANES), 1) // CHUNK
    no_q = jnp.zeros((HEAD_DIM, LANES), _BF)
    pairs = []
    for a in range(n_pairs):
        if banded:
            qc0 = pl.program_id(1) * (rb // CHUNK) + 2 * (a % pairs_per_stream)
            kc0 = jnp.maximum(qc0 - WINDOW_CHUNKS, 0)
            delta0 = kc0 - qc0
            kstart = pl.multiple_of(kc0 * CHUNK, LANES)
            vblk = kc0 // 2
        else:
            delta0, kstart, vblk = -WINDOW_CHUNKS, 0, 0
        rel = delta0 + key_blk - qry_blk
        cap_scr[a] = jnp.where((rel >= -WINDOW_CHUNKS) & (rel <= 0), MASK_PASS, NEG_INF)
        pairs.append((a // pairs_per_stream, kstart, vblk))

    def unit_heads(kv, gp):
        return (kv * Q_PER_KV + 2 * gp, kv * Q_PER_KV + 2 * gp + 1)

    def scores(a, kv, gp):
        lanes = slice(a * LANES, (a + 1) * LANES)
        stream, kstart, _ = pairs[a]
        k_tile = k_ref[stream, pl.ds(kstart, PAIR_KEYS), (kv // 2) * LANES:(kv // 2 + 1) * LANES]
        w_parts = []
        for h in unit_heads(kv, gp):
            q_h = qt_scr[h * HEAD_DIM:(h + 1) * HEAD_DIM, lanes]
            w_parts.append(jnp.concatenate([q_h, no_q] if kv % 2 == 0 else [no_q, q_h], axis=0))
        return _dot(k_tile, jnp.concatenate(w_parts, axis=1))

    def softmax_values(a, kv, gp, s):
        e_parts, inv_den = [], []
        for u, h in enumerate(unit_heads(kv, gp)):
            s_h = jnp.minimum(s[:, u * LANES:(u + 1) * LANES], cap_scr[a])
            sink = sinks_ref[h]
            m = jnp.maximum(_reduce_rows(s_h, jnp.maximum, jnp.max), sink)
            e = jnp.exp(s_h - m)
            inv_den.append(1.0 / (_reduce_rows(e, jnp.add, jnp.sum) + jnp.exp(sink - m)))
            e_parts.append(e.astype(_BF))
        stream, _, vblk = pairs[a]
        vt_kv = jnp.concatenate([vt_ref[stream, vblk, kv * HEAD_DIM:(kv + 1) * HEAD_DIM, :],
                                 vt_ref[stream, vblk + 1, kv * HEAD_DIM:(kv + 1) * HEAD_DIM, :]], axis=1)
        return _dot(vt_kv, jnp.concatenate(e_parts, axis=1)), inv_den

    def finish(a, kv, gp, ot, inv_den):
        lanes = slice(a * LANES, (a + 1) * LANES)
        for u, h in enumerate(unit_heads(kv, gp)):
            rows_h = slice(h * HEAD_DIM, (h + 1) * HEAD_DIM)
            ot_scr[rows_h, lanes] = ot[:, u * LANES:(u + 1) * LANES] * inv_den[u] * gt_scr[rows_h, lanes]

    n_k = D_MODEL // kt
    for c in (0, N_KV_HEADS):
        for k in range(n_k):
            in_piece(c, k)
        in_finish(c)
    slots = []
    for kv in range(N_KV_HEADS):
        units = [(a, kv, gp) for a in range(n_pairs) for gp in range(Q_PER_KV // 2)]
        ahead, behind = [], []
        if kv + 1 < N_KV_HEADS:
            for c in (kv + 1, kv + 1 + N_KV_HEADS):
                ahead += [functools.partial(in_piece, c, k) for k in range(n_k)] + [functools.partial(in_finish, c)]
        if kv > 0:
            behind = [functools.partial(out_prepare, kv - 1)] + [
                functools.partial(out_piece, kv - 1, n) for n in range(D_MODEL // HEAD_TILE)]
        work = [[] for _ in units]
        for m, item in enumerate(ahead):
            work[m * len(units) // len(ahead)].append(item)
        for m, item in enumerate(behind):
            work[1 + m * (len(units) - 1) // len(behind)].append(item)
        slots += list(zip(units, work))
    s_next, unfinished = scores(*slots[0][0]), None
    for i, (unit, work) in enumerate(slots):
        for item in work:
            item()
        s_cur = s_next
        if i + 1 < len(slots):
            s_next = scores(*slots[i + 1][0])
        ot, inv_den = softmax_values(*unit, s_cur)
        if unfinished is not None:
            finish(*unfinished)
        unfinished = (*unit, ot, inv_den)
    finish(*unfinished)
    out_prepare(N_KV_HEADS - 1)
    for n in range(D_MODEL // HEAD_TILE):
        out_piece(N_KV_HEADS - 1, n)

    r = DEEPNORM_ALPHA * x + gate_ref[...] * acc_scr[...].reshape(bb, rb, D_MODEL)
    y_ref[...] = _layer_norm(r, lng_ref[...], lnb_ref[...])


def _attn_layer(x, ada, win, wout, sinks, tables_t, k_bf, vt, lng, lnb, banded, bb, rb):
    bsz, seq, _ = x.shape
    rows = bb * rb
    keys = k_bf.shape[1]
    shift, scale, gate = (a.reshape(bsz, 1, D_MODEL) for a in ada)
    vec = lambda a: a.reshape(1, D_MODEL)
    row_spec = pl.BlockSpec((bb, rb, D_MODEL), lambda b, i: (b, i, 0))
    ada_spec = pl.BlockSpec((bb, 1, D_MODEL), lambda b, i: (b, 0, 0))
    tab_spec = pl.BlockSpec((HEAD_DIM // 2, rows), lambda b, i: (0, i))
    kern = functools.partial(_attn_layer_kernel, bb=bb, rb=rb, banded=banded)
    return pl.pallas_call(
        kern,
        grid=(bsz // bb, seq // rb),
        in_specs=[pl.BlockSpec(memory_space=pltpu.SMEM),
                  row_spec, ada_spec, ada_spec, ada_spec,
                  _const_spec((D_MODEL, 2 * D_MODEL)), _const_spec((D_MODEL, D_MODEL)),
                  tab_spec, tab_spec,
                  pl.BlockSpec((bb, keys, KV_WIDTH), lambda b, i: (b, 0, 0)),
                  pl.BlockSpec((bb, keys // LANES, KV_WIDTH, LANES), lambda b, i: (b, 0, 0, 0)),
                  _const_spec((1, D_MODEL)), _const_spec((1, D_MODEL))],
        out_specs=row_spec,
        out_shape=jax.ShapeDtypeStruct((bsz, seq, D_MODEL), _F32),
        scratch_shapes=[pltpu.VMEM((rows, D_MODEL), _BF),
                        pltpu.VMEM((2, rows, HEAD_TILE), _F32),
                        pltpu.VMEM((D_MODEL, rows), _BF),
                        pltpu.VMEM((D_MODEL, rows), _F32),
                        pltpu.VMEM((D_MODEL, rows), _F32),
                        pltpu.VMEM((rows, HEAD_TILE), _BF),
                        pltpu.VMEM((rows, D_MODEL), _F32),
                        pltpu.VMEM((rows // LANES, PAIR_KEYS, LANES), _F32)],
        compiler_params=pltpu.CompilerParams(dimension_semantics=("arbitrary", "arbitrary"),
                                             vmem_limit_bytes=VMEM_LIMIT),
        name="attn_layer",
    )(sinks, x, shift, scale, gate, win, wout, *tables_t, k_bf, vt, vec(lng), vec(lnb))


def _run_trunk(x, ada_all, pos, h0, cache_k, cache_v, p):
    bsz, seq, _ = x.shape
    ada = lambda l: tuple(ada_all[l][:, k * D_MODEL:(k + 1) * D_MODEL] for k in range(3))
    states = []
    for la in range(N_A_LAYERS):
        x, h_last = _s5_layer(x, bsz, seq, ada(la), h0[la], p['win_a'][la], p['bs'][la], p['cs'][la], p['ks'][la],
                              p['ar'][la], p['ai'][la], p['ssm_d'][la], p['w_glu'][la], p['b_glu'][la],
                              p['w_out_a'][la], p['ln_g'][la], p['ln_b'][la],
                              batch_major_in=la == 0, batch_major_out=la == N_A_LAYERS - 1)
        states.append(_strips_to_state(h_last))
    banded = cache_k is None
    outs = _shared_kv(x, p['w_kv'], _rope_tables(pos), transposed_v=banded)
    k_new, v_new, k_bf = outs[:3]
    if banded:
        vt = outs[3]
        q_pos = pos
    else:
        pad = LANES - seq
        flat = lambda c: c.reshape(bsz, c.shape[1], KV_WIDTH)
        k_bf = jnp.concatenate([flat(cache_k).astype(_BF), k_bf, jnp.zeros((bsz, pad, KV_WIDTH), _BF)], axis=1)
        v_all = jnp.concatenate([flat(cache_v), v_new, jnp.zeros((bsz, pad, KV_WIDTH), _F32)], axis=1)
        vt = v_all.reshape(bsz, PAIR_KEYS // LANES, LANES, KV_WIDTH).transpose(0, 1, 3, 2).astype(_BF)
        x = jnp.pad(x, ((0, 0), (0, pad), (0, 0)))
        q_pos = jnp.tile(pos[0] + jnp.arange(LANES, dtype=jnp.int32), bsz)
    cos, sin = _rope_angles(q_pos)
    tables_t = (cos.T, sin.T)
    bb, rb = (1, ATTN_ROWS) if banded else (bsz, LANES)
    for lb in range(N_B_LAYERS):
        layer = N_A_LAYERS + lb
        x = _attn_layer(x, ada(layer), p['win_b'][lb], p['w_out_b'][lb], p['attn_sinks'][lb], tables_t, k_bf, vt,
                        p['ln_g'][layer], p['ln_b'][layer], banded=banded, bb=bb, rb=rb)
    k4 = k_new.reshape(bsz, seq, N_KV_HEADS, HEAD_DIM)
    v4 = v_new.reshape(bsz, seq, N_KV_HEADS, HEAD_DIM)
    return x[:, :seq], jnp.stack(states), k4, v4


def kernel(x_prompt, x_sample, state_ssm, cache_k, cache_v, c_prompt, c_sample, w_ada, b_ada, ln_g, ln_b, w_in_a,
           ssm_a_re, ssm_a_im, ssm_b_re, ssm_b_im, ssm_c_re, ssm_c_im, ssm_d, ssm_log_dt, w_glu, b_glu, w_out_a,
           w_kv, w_in_b, attn_sinks, w_out_b):
    n_prompt, n_sample = x_prompt.shape[0], x_sample.shape[0]
    ada_all = _ada_params(jnp.concatenate([c_prompt, c_sample], axis=0), w_ada, b_ada)
    bs, cs, ks, ar, ai = _s5_matrices(
        _s5_prep(ssm_a_re, ssm_a_im, ssm_log_dt, ssm_b_re, ssm_b_im, ssm_c_re, ssm_c_im))
    p = dict(win_a=w_in_a.astype(_BF), bs=bs, cs=cs, ks=ks, ar=ar, ai=ai, ssm_d=ssm_d, w_glu=w_glu.astype(_BF),
             b_glu=b_glu, w_out_a=w_out_a.astype(_BF), ln_g=ln_g, ln_b=ln_b, w_kv=w_kv.astype(_BF),
             win_b=w_in_b.astype(_BF), w_out_b=w_out_b.astype(_BF), attn_sinks=attn_sinks)

    pos_prompt = jnp.arange(x_prompt.shape[1], dtype=jnp.int32)
    pos_sample = PAST_LEN + jnp.arange(x_sample.shape[1], dtype=jnp.int32)
    h0_prompt = jnp.zeros((N_A_LAYERS, N_STRIPS, n_prompt, STRIP_W), _F32)
    h0_sample = jnp.stack([_state_to_strips(state_ssm[la]) for la in range(N_A_LAYERS)])

    y_p, ssm_p, k_p, v_p = _run_trunk(x_prompt, ada_all[:, :n_prompt], pos_prompt, h0_prompt, None, None, p)
    y_s, ssm_s, k_s, v_s = _run_trunk(x_sample, ada_all[:, n_prompt:], pos_sample, h0_sample, cache_k, cache_v, p)
    rows = min(WINDOW, x_prompt.shape[1])
    return (y_p, y_s, ssm_p, k_p[:, -rows:], v_p[:, -rows:], ssm_s, k_s, v_s)
```

```python
import functools
import math

import jax
import jax.numpy as jnp
from jax import lax
from jax.experimental import pallas as pl
from jax.experimental.pallas import tpu as pltpu

D_MODEL = 1024
DEPTH = 4
CHUNK = 64
N_A_LAYERS = 2
N_B_LAYERS = 2
SSM_GROUP_CH = 16
SSM_GROUPS = 64
SSM_STATE = 64
HEAD_DIM = 64
N_HEADS = 16
N_KV_HEADS = 4
Q_PER_KV = 4
KV_WIDTH = N_KV_HEADS * HEAD_DIM
WINDOW = 128
WINDOW_CHUNKS = WINDOW // CHUNK
PAST_LEN = 1024
ROPE_THETA = 10000.0
NEG_INF = -1e30
DEEPNORM_ALPHA = (2.0 * DEPTH) ** 0.25
LN_EPS = 1e-5

LANES = 128
STRIP_GROUPS = LANES // SSM_GROUP_CH
N_STRIPS = SSM_GROUPS // STRIP_GROUPS
STRIP_HALF = STRIP_GROUPS * SSM_STATE
STRIP_W = 2 * STRIP_HALF
SCAN_W = 256
S5_ROWS = 512
KEYS = (WINDOW_CHUNKS + 1) * CHUNK
PAIR_KEYS = KEYS + CHUNK
ATTN_ROWS = 512
HEAD_TILE = Q_PER_KV * HEAD_DIM
IN_PIECE = 256
KV_ROWS = 512
MASK_PASS = 3.0e38
LOG2_E = math.log2(math.e)
VMEM_LIMIT = 56 * 1024 * 1024

_BF = jnp.bfloat16
_F32 = jnp.float32


def _dot(a, b):
    return jnp.dot(a, b, preferred_element_type=_F32)


def _const_spec(shape):
    nd = len(shape)
    return pl.BlockSpec(shape, lambda *_: (0,) * nd, pipeline_mode=pl.Buffered(1))


def _layer_norm(r, g, b):
    mu = jnp.mean(r, axis=-1, keepdims=True)
    d = r - mu
    var = jnp.mean(d * d, axis=-1, keepdims=True)
    return d * lax.rsqrt(var + LN_EPS) * g + b


def _ada_kernel(c_ref, w_ref, b_ref, o_ref):
    c = c_ref[...]
    o_ref[0] = _dot(jax.nn.silu(c).astype(_BF), w_ref[0].astype(_BF)) + b_ref[0]


def _ada_params(c_all, w_ada, b_ada):
    n = c_all.shape[0]
    return pl.pallas_call(
        _ada_kernel,
        grid=(DEPTH, 3),
        in_specs=[pl.BlockSpec((n, D_MODEL), lambda l, j: (0, 0)),
                  pl.BlockSpec((1, D_MODEL, D_MODEL), lambda l, j: (l, 0, j)),
                  pl.BlockSpec((1, 1, D_MODEL), lambda l, j: (l, 0, j))],
        out_specs=pl.BlockSpec((1, n, D_MODEL), lambda l, j: (l, 0, j)),
        out_shape=jax.ShapeDtypeStruct((DEPTH, n, 3 * D_MODEL), _F32),
        name="ada_params",
    )(c_all, w_ada, b_ada.reshape(DEPTH, 1, 3 * D_MODEL))


def _cmul(x_re, x_im, y_re, y_im):
    return x_re * y_re - x_im * y_im, x_re * y_im + x_im * y_re


def _zoh_a(a_re, a_im, log_dt):
    dt = jnp.exp(log_dt)
    mag = jnp.exp(a_re * dt)
    return mag * jnp.cos(a_im * dt), mag * jnp.sin(a_im * dt)


def _s5_prep_kernel(are_ref, aim_ref, ldt_ref, bre_ref, bim_ref, aret_ref, aimt_ref, ldtt_ref, cre_ref, cim_ref,
                    a2r_ref, a2i_ref, bbr_ref, bbi_ref, abr_ref, abi_ref, car_ref, cai_ref, ca2r_ref, ca2i_ref):
    a_re, a_im = are_ref[0], aim_ref[0]
    ab_re, ab_im = _zoh_a(a_re, a_im, ldt_ref[0])
    nr, ni = ab_re - 1.0, ab_im
    den = a_re * a_re + a_im * a_im
    f_re = (nr * a_re + ni * a_im) / den
    f_im = (ni * a_re - nr * a_im) / den
    bb_re, bb_im = _cmul(f_re, f_im, bre_ref[0], bim_ref[0])
    a2r_ref[0], a2i_ref[0] = _cmul(ab_re, ab_im, ab_re, ab_im)
    bbr_ref[0], bbi_ref[0] = bb_re, bb_im
    abr_ref[0], abi_ref[0] = _cmul(ab_re, ab_im, bb_re, bb_im)
    at_re, at_im = _zoh_a(aret_ref[0], aimt_ref[0], ldtt_ref[0])
    ca_re, ca_im = _cmul(cre_ref[0], cim_ref[0], at_re, at_im)
    car_ref[0], cai_ref[0] = ca_re, ca_im
    ca2r_ref[0], ca2i_ref[0] = _cmul(ca_re, ca_im, at_re, at_im)


def _s5_lag_kernel(cr_ref, ci_ref, car_ref, cai_ref, bbr_ref, bbi_ref, k0_ref, k1_ref):
    def re_prod(x_re, x_im):
        dot = lambda x, y: jnp.einsum('gcp,gpd->gcd', x, y, precision=lax.Precision.HIGHEST,
                                      preferred_element_type=_F32)
        return dot(x_re, bbr_ref[0]) - dot(x_im, bbi_ref[0])
    k0_ref[0] = re_prod(cr_ref[0], ci_ref[0])
    k1_ref[0] = re_prod(car_ref[0], cai_ref[0])


def _s5_prep(a_re, a_im, log_dt, b_re, b_im, c_re, c_im):
    nl, w = N_A_LAYERS, SSM_STATE * SSM_GROUP_CH
    flat = lambda a: a.reshape(nl, SSM_GROUPS, w)
    rep = lambda a: jnp.repeat(a, SSM_GROUP_CH, axis=-1)
    til = lambda a: jnp.tile(a, (1, 1, SSM_GROUP_CH))
    ldt = log_dt[:, :, None]
    spec = pl.BlockSpec((1, SSM_GROUPS, w), lambda l: (l, 0, 0))
    shp = jax.ShapeDtypeStruct((nl, SSM_GROUPS, w), _F32)
    a2_re, a2_im, bb_re, bb_im, abb_re, abb_im, ca_re, ca_im, ca2_re, ca2_im = pl.pallas_call(
        _s5_prep_kernel, grid=(nl,), in_specs=[spec] * 10, out_specs=[spec] * 10, out_shape=[shp] * 10, name="s5_prep",
    )(rep(a_re), rep(a_im), rep(jnp.broadcast_to(ldt, a_re.shape)), flat(b_re), flat(b_im),
      til(a_re), til(a_im), til(jnp.broadcast_to(ldt, a_re.shape)), flat(c_re), flat(c_im))
    pc = (nl, SSM_GROUPS, SSM_STATE, SSM_GROUP_CH)
    cp = (nl, SSM_GROUPS, SSM_GROUP_CH, SSM_STATE)
    cp_spec = pl.BlockSpec((1,) + cp[1:], lambda l: (l, 0, 0, 0))
    pc_spec = pl.BlockSpec((1,) + pc[1:], lambda l: (l, 0, 0, 0))
    kk = (nl, SSM_GROUPS, SSM_GROUP_CH, SSM_GROUP_CH)
    kk_spec = pl.BlockSpec((1,) + kk[1:], lambda l: (l, 0, 0, 0))
    k0, k1 = pl.pallas_call(
        _s5_lag_kernel, grid=(nl,), in_specs=[cp_spec] * 4 + [pc_spec] * 2, out_specs=[kk_spec] * 2,
        out_shape=[jax.ShapeDtypeStruct(kk, _F32)] * 2, name="s5_lag",
    )(c_re, c_im, ca_re.reshape(cp), ca_im.reshape(cp), bb_re.reshape(pc), bb_im.reshape(pc))
    return dict(a2_re=a2_re.reshape(pc)[..., 0], a2_im=a2_im.reshape(pc)[..., 0],
                bb_re=bb_re.reshape(pc), bb_im=bb_im.reshape(pc), abb_re=abb_re.reshape(pc), abb_im=abb_im.reshape(pc),
                ca_re=ca_re.reshape(cp), ca_im=ca_im.reshape(cp), ca2_re=ca2_re.reshape(cp), ca2_im=ca2_im.reshape(cp),
                k0=k0, k1=k1)


def _s5_matrices(q):
    eye = jnp.eye(STRIP_GROUPS, dtype=_F32)
    nl = N_A_LAYERS

    def expand_b(re, im):
        bb = jnp.stack([re, im], axis=1).reshape(nl, 2, N_STRIPS, STRIP_GROUPS, SSM_STATE, SSM_GROUP_CH)
        return jnp.einsum('lrjgpc,gh->ljgcrhp', bb, eye).reshape(nl, N_STRIPS, LANES, STRIP_W)

    def project_c(re, im):
        cc = jnp.stack([re, -im], axis=1).reshape(nl, 2, N_STRIPS, STRIP_GROUPS, SSM_GROUP_CH, SSM_STATE)
        return jnp.einsum('lrjgcp,gh->ljrhpgc', cc, eye).reshape(nl, N_STRIPS, STRIP_W, LANES)

    def lag(k):
        kt = k.transpose(0, 1, 3, 2).reshape(nl, N_STRIPS, STRIP_GROUPS, SSM_GROUP_CH, SSM_GROUP_CH)
        return jnp.einsum('ljgdc,gh->ljgdhc', kt, eye).reshape(nl, N_STRIPS, LANES, LANES)

    bs2 = jnp.concatenate([expand_b(q['abb_re'], q['abb_im']), expand_b(q['bb_re'], q['bb_im'])], axis=2)
    cs2 = jnp.concatenate([project_c(q['ca_re'], q['ca_im']), project_c(q['ca2_re'], q['ca2_im'])], axis=3)
    k0, k1 = lag(q['k0']), lag(q['k1'])
    ks = jnp.concatenate([jnp.concatenate([k0, k1], axis=3),
                          jnp.concatenate([jnp.zeros_like(k0), k0], axis=3)], axis=2)
    a2r = q['a2_re'].reshape(nl, N_STRIPS, 1, STRIP_HALF)
    a2i = q['a2_im'].reshape(nl, N_STRIPS, 1, STRIP_HALF)
    return bs2.astype(_BF), cs2.astype(_BF), ks.astype(_BF), a2r, a2i


def _state_to_strips(state):
    nb = state.shape[0]
    s = state.reshape(nb, N_STRIPS, STRIP_GROUPS, SSM_STATE, 2)
    return s.transpose(1, 0, 4, 2, 3).reshape(N_STRIPS, nb, STRIP_W)


def _strips_to_state(h):
    nb = h.shape[1]
    s = h.reshape(N_STRIPS, nb, 2, STRIP_GROUPS, SSM_STATE)
    return s.transpose(1, 0, 3, 4, 2).reshape(nb, SSM_GROUPS, SSM_STATE, 2)


def _s5_layer_kernel(x_ref, shift_ref, scale_ref, gate_ref, h0_ref, win_ref, bs_ref, cs_ref, ks_ref, ar_ref, ai_ref,
                     d_ref, wglu_ref, bglu_ref, wout_ref, lng_ref, lnb_ref,
                     y_ref, hout_ref,
                     h_scr, u_scr, z_scr, bu_scr, yy_scr, xt_scr, *, nb, steps, batch_major_in, batch_major_out):
    rows = nb * steps
    i = pl.program_id(0)

    @pl.when(i == 0)
    def _():
        h_scr[...] = h0_ref[...]

    lane_tiles = D_MODEL // LANES
    if batch_major_in:
        for b in range(nb):
            for c in range(lane_tiles):
                xt_scr[c, pl.ds(b, steps, stride=nb), :] = x_ref[b, :, c * LANES:(c + 1) * LANES]
        x_rows = jnp.concatenate([xt_scr[c] for c in range(lane_tiles)], axis=1)
    else:
        x_rows = x_ref[...]
    x3 = x_rows.reshape(steps, nb, D_MODEL)
    hmod = (x3 * (1.0 + scale_ref[...])[None] + shift_ref[...][None]).reshape(rows, D_MODEL)
    uz = _dot(hmod.astype(_BF), win_ref[...])
    pairs = steps // 2
    prow = pairs * nb
    for j in range(N_STRIPS):
        u_scr[j] = uz[:, j * LANES:(j + 1) * LANES].reshape(pairs, 2, nb, LANES)
    z_scr[...] = uz[:, D_MODEL:]

    def pair_inputs(j):
        return jnp.concatenate([u_scr[j, :, 0].reshape(prow, LANES), u_scr[j, :, 1].reshape(prow, LANES)], axis=1)

    def expand(j):
        bu_scr[j % 2] = _dot(pair_inputs(j).astype(_BF), bs_ref[j])

    def scan(j):
        buf = j % 2
        a_re_row, a_im_row = ar_ref[j], ai_ref[j]
        for s in range(STRIP_HALF // SCAN_W):
            re = slice(s * SCAN_W, (s + 1) * SCAN_W)
            im = slice(STRIP_HALF + s * SCAN_W, STRIP_HALF + (s + 1) * SCAN_W)
            a_re = jnp.broadcast_to(a_re_row[:, re], (nb, SCAN_W))
            a_im = jnp.broadcast_to(a_im_row[:, re], (nb, SCAN_W))
            h_re, h_im = h_scr[j, :, re], h_scr[j, :, im]
            for t in range(pairs):
                now = slice(t * nb, (t + 1) * nb)
                n_re = a_re * h_re - a_im * h_im + bu_scr[buf, now, re]
                n_im = a_re * h_im + a_im * h_re + bu_scr[buf, now, im]
                bu_scr[buf, now, re] = h_re
                bu_scr[buf, now, im] = h_im
                h_re, h_im = n_re, n_im
            h_scr[j, :, re] = h_re
            h_scr[j, :, im] = h_im

    def project(j):
        yy = _dot(bu_scr[j % 2].astype(_BF), cs_ref[j]) + _dot(pair_inputs(j).astype(_BF), ks_ref[j])
        for par in range(2):
            y_par = yy[:, par * LANES:(par + 1) * LANES].reshape(pairs, nb, LANES) + d_ref[j] * u_scr[j, :, par]
            yy_scr[j, :, par] = y_par

    expand(0)
    for j in range(N_STRIPS):
        if j + 1 < N_STRIPS:
            expand(j + 1)
        scan(j)
        project(j)

    y = jnp.concatenate([yy_scr[j].reshape(rows, LANES) for j in range(N_STRIPS)], axis=1)
    g = jax.nn.gelu(y)
    y = g * jax.nn.sigmoid(_dot(g.astype(_BF), wglu_ref[...]) + bglu_ref[...])
    y = y * jax.nn.silu(z_scr[...])
    out = _dot(y.astype(_BF), wout_ref[...]).reshape(steps, nb, D_MODEL)
    r = (DEEPNORM_ALPHA * x3 + gate_ref[...][None] * out).reshape(rows, D_MODEL)
    y = _layer_norm(r, lng_ref[...], lnb_ref[...])
    if batch_major_out:
        for c in range(lane_tiles):
            xt_scr[c] = y[:, c * LANES:(c + 1) * LANES]
        for b in range(nb):
            for c in range(lane_tiles):
                y_ref[b, :, c * LANES:(c + 1) * LANES] = xt_scr[c, pl.ds(b, steps, stride=nb), :]
    else:
        y_ref[...] = y

    @pl.when(i == pl.num_programs(0) - 1)
    def _():
        hout_ref[...] = h_scr[...]


def _s5_layer(x, nb, seq, ada, h0, win, bs, cs, ks, ar, ai, d, wglu, bglu, wout, lng, lnb, batch_major_in,
              batch_major_out):
    total = seq * nb
    rows = min(S5_ROWS, total)
    steps = rows // nb
    shift, scale, gate = ada
    tb_spec = pl.BlockSpec((rows, D_MODEL), lambda i: (i, 0))
    bt_spec = pl.BlockSpec((nb, steps, D_MODEL), lambda i: (0, i, 0))
    tb_shape = jax.ShapeDtypeStruct((total, D_MODEL), _F32)
    bt_shape = jax.ShapeDtypeStruct((nb, seq, D_MODEL), _F32)
    vec = lambda a: a.reshape(1, D_MODEL)
    kern = functools.partial(_s5_layer_kernel, nb=nb, steps=steps, batch_major_in=batch_major_in,
                             batch_major_out=batch_major_out)
    return pl.pallas_call(
        kern,
        grid=(total // rows,),
        in_specs=[bt_spec if batch_major_in else tb_spec,
                  _const_spec((nb, D_MODEL)), _const_spec((nb, D_MODEL)), _const_spec((nb, D_MODEL)),
                  _const_spec((N_STRIPS, nb, STRIP_W)),
                  _const_spec((D_MODEL, 2 * D_MODEL)),
                  _const_spec((N_STRIPS, 2 * LANES, STRIP_W)),
                  _const_spec((N_STRIPS, STRIP_W, 2 * LANES)),
                  _const_spec((N_STRIPS, 2 * LANES, 2 * LANES)),
                  _const_spec((N_STRIPS, 1, STRIP_HALF)), _const_spec((N_STRIPS, 1, STRIP_HALF)),
                  _const_spec((N_STRIPS, 1, LANES)),
                  _const_spec((D_MODEL, D_MODEL)), _const_spec((1, D_MODEL)),
                  _const_spec((D_MODEL, D_MODEL)),
                  _const_spec((1, D_MODEL)), _const_spec((1, D_MODEL))],
        out_specs=[bt_spec if batch_major_out else tb_spec,
                   pl.BlockSpec((N_STRIPS, nb, STRIP_W), lambda i: (0, 0, 0))],
        out_shape=[bt_shape if batch_major_out else tb_shape,
                   jax.ShapeDtypeStruct((N_STRIPS, nb, STRIP_W), _F32)],
        scratch_shapes=[pltpu.VMEM((N_STRIPS, nb, STRIP_W), _F32),
                        pltpu.VMEM((N_STRIPS, steps // 2, 2, nb, LANES), _F32),
                        pltpu.VMEM((rows, D_MODEL), _F32),
                        pltpu.VMEM((2, rows // 2, STRIP_W), _F32),
                        pltpu.VMEM((N_STRIPS, steps // 2, 2, nb, LANES), _F32),
                        pltpu.VMEM((D_MODEL // LANES, rows, LANES), _F32)],
        compiler_params=pltpu.CompilerParams(dimension_semantics=("arbitrary",), vmem_limit_bytes=VMEM_LIMIT),
        name="s5_layer",
    )(x, shift, scale, gate, h0, win, bs, cs, ks, ar, ai, d.reshape(N_STRIPS, 1, LANES),
      wglu, vec(bglu), wout, vec(lng), vec(lnb))


def _rope_angles(pos):
    half = HEAD_DIM // 2
    inv_freq = jnp.power(ROPE_THETA, -jnp.arange(half, dtype=_F32) / half)
    ang = pos.astype(_F32)[:, None] * inv_freq[None, :]
    return jnp.cos(ang), jnp.sin(ang)


def _rope_tables(pos):
    cos, sin = _rope_angles(pos)
    zero = jnp.zeros_like(sin)
    cos_t = jnp.tile(cos, (1, LANES // (HEAD_DIM // 2)))
    sin_lo = jnp.tile(jnp.concatenate([-sin, zero], axis=1), (1, LANES // HEAD_DIM))
    sin_hi = jnp.tile(jnp.concatenate([zero, sin], axis=1), (1, LANES // HEAD_DIM))
    return cos_t, sin_lo, sin_hi


def _rope_tile(x, cos_t, sin_lo, sin_hi):
    half = HEAD_DIM // 2
    return x * cos_t + pltpu.roll(x, LANES - half, 1) * sin_lo + pltpu.roll(x, half, 1) * sin_hi


def _kv_kernel(x_ref, w_ref, cos_ref, slo_ref, shi_ref, k_ref, v_ref, kbf_ref):
    kv = _dot(x_ref[0].astype(_BF), w_ref[...])
    cos_t, sin_lo, sin_hi = cos_ref[...], slo_ref[...], shi_ref[...]
    for t in range(KV_WIDTH // LANES):
        tile = _rope_tile(kv[:, t * LANES:(t + 1) * LANES], cos_t, sin_lo, sin_hi)
        k_ref[0, :, t * LANES:(t + 1) * LANES] = tile
        kbf_ref[0, :, t * LANES:(t + 1) * LANES] = tile.astype(_BF)
    v_ref[0] = kv[:, KV_WIDTH:]


def _shared_kv(x, w_kv, tables):
    bsz, seq, _ = x.shape
    rows = min(KV_ROWS, seq)
    tab_spec = pl.BlockSpec((rows, LANES), lambda b, i: (i, 0))
    kv_spec = pl.BlockSpec((1, rows, KV_WIDTH), lambda b, i: (b, i, 0))
    out_specs = [kv_spec] * 3
    out_shape = ([jax.ShapeDtypeStruct((bsz, seq, KV_WIDTH), _F32)] * 2
                 + [jax.ShapeDtypeStruct((bsz, seq, KV_WIDTH), _BF)])
    return pl.pallas_call(
        _kv_kernel,
        grid=(bsz, seq // rows),
        in_specs=[pl.BlockSpec((1, rows, D_MODEL), lambda b, i: (b, i, 0)),
                  _const_spec((D_MODEL, 2 * KV_WIDTH)), tab_spec, tab_spec, tab_spec],
        out_specs=out_specs, out_shape=out_shape,
        compiler_params=pltpu.CompilerParams(dimension_semantics=("arbitrary", "arbitrary"),
                                             vmem_limit_bytes=VMEM_LIMIT),
        name="shared_kv",
    )(x, w_kv, *tables)


def _reduce_rows(x, pair_op, final_op):
    sub = 8
    parts = [x[i * sub:(i + 1) * sub] for i in range(x.shape[0] // sub)]
    while len(parts) > 1:
        parts = [pair_op(parts[i], parts[i + 1]) for i in range(0, len(parts), 2)]
    return final_op(parts[0], axis=0, keepdims=True)


def _attn_layer_kernel(sinks_ref, x_ref, shift_ref, scale_ref, gate_ref, win_ref, wout_ref, cos_ref, sin_ref,
                       lng_ref, lnb_ref, *refs, bb, rb, banded, make_kv):
    rows = bb * rb
    n_pairs = rows // LANES
    pairs_per_stream = rb // LANES
    x = x_ref[...]
    if make_kv:
        (wkv_ref, cosr_ref, slo_ref, shi_ref, y_ref, kcache_ref, vcache_ref, kbf_ref, vtout_ref,
         hm_scr, blk_scr, qt_scr, gt_scr, ot_scr, ob_scr, acc_scr, cap_scr, k_src, vt_src) = refs
        step = pl.program_id(1)
        kv_new = _dot(x.reshape(rows, D_MODEL).astype(_BF), wkv_ref[...])
        first_row = pl.multiple_of(step * rb, rb)
        for t in range(KV_WIDTH // LANES):
            tile = _rope_tile(kv_new[:, t * LANES:(t + 1) * LANES], cosr_ref[...], slo_ref[...], shi_ref[...])
            kcache_ref[0, :, t * LANES:(t + 1) * LANES] = tile[rb - WINDOW:]
            kbf_ref[0, :, t * LANES:(t + 1) * LANES] = tile.astype(_BF)
            k_src[0, pl.ds(first_row, rb), t * LANES:(t + 1) * LANES] = tile.astype(_BF)
        v_new = kv_new[:, KV_WIDTH:]
        vcache_ref[0] = v_new[rb - WINDOW:]
        for blk in range(rb // LANES):
            vt_blk = v_new[blk * LANES:(blk + 1) * LANES].T.astype(_BF)
            vtout_ref[0, blk] = vt_blk
            vt_src[0, step * (rb // LANES) + blk] = vt_blk
    else:
        (k_src, vt_src, y_ref, hm_scr, blk_scr, qt_scr, gt_scr, ot_scr, ob_scr, acc_scr, cap_scr) = refs
    hm_scr[...] = (x * (1.0 + scale_ref[...]) + shift_ref[...]).reshape(rows, D_MODEL).astype(_BF)
    cos_t, sin_t = cos_ref[...], sin_ref[...]
    half = HEAD_DIM // 2
    qk_scale = HEAD_DIM ** -0.5 * LOG2_E
    kt = IN_PIECE

    def in_piece(c, k):
        part = _dot(hm_scr[:, k * kt:(k + 1) * kt], win_ref[k * kt:(k + 1) * kt, c * HEAD_TILE:(c + 1) * HEAD_TILE])
        if k == 0:
            blk_scr[c // N_KV_HEADS] = part
        else:
            blk_scr[c // N_KV_HEADS] += part

    def in_finish(c):
        t = blk_scr[c // N_KV_HEADS].T
        if c < N_KV_HEADS:
            for j in range(Q_PER_KV):
                lo = c * HEAD_TILE + j * HEAD_DIM
                x1, x2 = t[j * HEAD_DIM:j * HEAD_DIM + half], t[j * HEAD_DIM + half:(j + 1) * HEAD_DIM]
                qt_scr[lo:lo + half] = ((x1 * cos_t - x2 * sin_t) * qk_scale).astype(_BF)
                qt_scr[lo + half:lo + HEAD_DIM] = ((x2 * cos_t + x1 * sin_t) * qk_scale).astype(_BF)
        else:
            lo = (c - N_KV_HEADS) * HEAD_TILE
            gt_scr[lo:lo + HEAD_TILE] = jax.nn.silu(t)

    def out_prepare(kv):
        ob_scr[...] = ot_scr[kv * HEAD_TILE:(kv + 1) * HEAD_TILE].T.astype(_BF)

    def out_piece(kv, n):
        part = _dot(ob_scr[...], wout_ref[kv * HEAD_TILE:(kv + 1) * HEAD_TILE, n * HEAD_TILE:(n + 1) * HEAD_TILE])
        if kv == 0:
            acc_scr[:, n * HEAD_TILE:(n + 1) * HEAD_TILE] = part
        else:
            acc_scr[:, n * HEAD_TILE:(n + 1) * HEAD_TILE] += part

    key_blk = lax.broadcasted_iota(jnp.int32, (PAIR_KEYS, LANES), 0) // CHUNK
    qry_blk = lax.broadcasted_iota(jnp.int32, (PAIR_KEYS, LANES), 1) // CHUNK
    no_q = jnp.zeros((HEAD_DIM, LANES), _BF)
    pairs = []
    for a in range(n_pairs):
        if banded:
            qc0 = pl.program_id(1) * (rb // CHUNK) + 2 * (a % pairs_per_stream)
            kc0 = jnp.maximum(qc0 - WINDOW_CHUNKS, 0)
            delta0 = kc0 - qc0
            kstart = pl.multiple_of(kc0 * CHUNK, LANES)
            vblk = kc0 // 2
        else:
            delta0, kstart, vblk = -WINDOW_CHUNKS, 0, 0
        rel = delta0 + key_blk - qry_blk
        cap_scr[a] = jnp.where((rel >= -WINDOW_CHUNKS) & (rel <= 0), MASK_PASS, NEG_INF)
        pairs.append((a // pairs_per_stream, kstart, vblk))

    def unit_heads(kv, gp):
        return (kv * Q_PER_KV + 2 * gp, kv * Q_PER_KV + 2 * gp + 1)

    def scores(a, kv, gp):
        lanes = slice(a * LANES, (a + 1) * LANES)
        stream, kstart, _ = pairs[a]
        k_tile = k_src[stream, pl.ds(kstart, PAIR_KEYS), (kv // 2) * LANES:(kv // 2 + 1) * LANES]
        w_parts = []
        for h in unit_heads(kv, gp):
            q_h = qt_scr[h * HEAD_DIM:(h + 1) * HEAD_DIM, lanes]
            w_parts.append(jnp.concatenate([q_h, no_q] if kv % 2 == 0 else [no_q, q_h], axis=0))
        return _dot(k_tile, jnp.concatenate(w_parts, axis=1))

    def softmax_values(a, kv, gp, s):
        e_parts, inv_den = [], []
        for u, h in enumerate(unit_heads(kv, gp)):
            s_h = jnp.minimum(s[:, u * LANES:(u + 1) * LANES], cap_scr[a])
            sink = sinks_ref[h] * LOG2_E
            m = jnp.maximum(_reduce_rows(s_h, jnp.maximum, jnp.max), sink)
            e = jnp.exp2(s_h - m)
            inv_den.append(1.0 / (_reduce_rows(e, jnp.add, jnp.sum) + jnp.exp2(sink - m)))
            e_parts.append(e.astype(_BF))
        stream, _, vblk = pairs[a]
        vt_kv = jnp.concatenate([vt_src[stream, vblk, kv * HEAD_DIM:(kv + 1) * HEAD_DIM, :],
                                 vt_src[stream, vblk + 1, kv * HEAD_DIM:(kv + 1) * HEAD_DIM, :]], axis=1)
        return _dot(vt_kv, jnp.concatenate(e_parts, axis=1)), inv_den

    def finish(a, kv, gp, ot, inv_den):
        lanes = slice(a * LANES, (a + 1) * LANES)
        for u, h in enumerate(unit_heads(kv, gp)):
            rows_h = slice(h * HEAD_DIM, (h + 1) * HEAD_DIM)
            ot_scr[rows_h, lanes] = ot[:, u * LANES:(u + 1) * LANES] * inv_den[u] * gt_scr[rows_h, lanes]

    n_k = D_MODEL // kt
    for c in (0, N_KV_HEADS):
        for k in range(n_k):
            in_piece(c, k)
        in_finish(c)
    slots = []
    for kv in range(N_KV_HEADS):
        units = [(a, kv, gp) for a in range(n_pairs) for gp in range(Q_PER_KV // 2)]
        ahead, behind = [], []
        if kv + 1 < N_KV_HEADS:
            for c in (kv + 1, kv + 1 + N_KV_HEADS):
                ahead += [functools.partial(in_piece, c, k) for k in range(n_k)] + [functools.partial(in_finish, c)]
        if kv > 0:
            behind = [functools.partial(out_prepare, kv - 1)] + [
                functools.partial(out_piece, kv - 1, n) for n in range(D_MODEL // HEAD_TILE)]
        work = [[] for _ in units]
        for m, item in enumerate(ahead):
            work[m * len(units) // len(ahead)].append(item)
        for m, item in enumerate(behind):
            work[1 + m * (len(units) - 1) // len(behind)].append(item)
        slots += list(zip(units, work))
    s_next, unfinished = scores(*slots[0][0]), None
    for i, (unit, work) in enumerate(slots):
        for item in work:
            item()
        s_cur = s_next
        if i + 1 < len(slots):
            s_next = scores(*slots[i + 1][0])
        ot, inv_den = softmax_values(*unit, s_cur)
        if unfinished is not None:
            finish(*unfinished)
        unfinished = (*unit, ot, inv_den)
    finish(*unfinished)
    out_prepare(N_KV_HEADS - 1)
    for n in range(D_MODEL // HEAD_TILE):
        out_piece(N_KV_HEADS - 1, n)

    r = DEEPNORM_ALPHA * x + gate_ref[...] * acc_scr[...].reshape(bb, rb, D_MODEL)
    y_ref[...] = _layer_norm(r, lng_ref[...], lnb_ref[...])


def _attn_layer(x, ada, win, wout, sinks, tables_t, lng, lnb, banded, bb, rb, kv=None, kv_maker=None):
    bsz, seq, _ = x.shape
    rows = bb * rb
    shift, scale, gate = (a.reshape(bsz, 1, D_MODEL) for a in ada)
    vec = lambda a: a.reshape(1, D_MODEL)
    row_spec = pl.BlockSpec((bb, rb, D_MODEL), lambda b, i: (b, i, 0))
    ada_spec = pl.BlockSpec((bb, 1, D_MODEL), lambda b, i: (b, 0, 0))
    tab_spec = pl.BlockSpec((HEAD_DIM // 2, rows), lambda b, i: (0, i))
    y_shape = jax.ShapeDtypeStruct((bsz, seq, D_MODEL), _F32)
    kern = functools.partial(_attn_layer_kernel, bb=bb, rb=rb, banded=banded, make_kv=kv is None)
    if kv is None:
        assert bb == 1
        w_kv, row_tables = kv_maker
        rtab_spec = pl.BlockSpec((rb, LANES), lambda b, i: (i, 0))
        kv_in, kv_specs = (w_kv, *row_tables), [_const_spec((D_MODEL, 2 * KV_WIDTH))] + [rtab_spec] * 3
        cache_spec = pl.BlockSpec((1, WINDOW, KV_WIDTH), lambda b, i: (b, 0, 0))
        out_specs = [row_spec, cache_spec, cache_spec,
                     pl.BlockSpec((1, rb, KV_WIDTH), lambda b, i: (b, i, 0)),
                     pl.BlockSpec((1, rb // LANES, KV_WIDTH, LANES), lambda b, i: (b, i, 0, 0))]
        out_shape = [y_shape] + [jax.ShapeDtypeStruct((bsz, WINDOW, KV_WIDTH), _F32)] * 2 + [
            jax.ShapeDtypeStruct((bsz, seq, KV_WIDTH), _BF),
            jax.ShapeDtypeStruct((bsz, seq // LANES, KV_WIDTH, LANES), _BF)]
        kv_scratch = [pltpu.VMEM((1, seq, KV_WIDTH), _BF), pltpu.VMEM((1, seq // LANES, KV_WIDTH, LANES), _BF)]
    else:
        keys = kv[0].shape[1]
        kv_in, kv_specs = kv, [pl.BlockSpec((bb, keys, KV_WIDTH), lambda b, i: (b, 0, 0)),
                               pl.BlockSpec((bb, keys // LANES, KV_WIDTH, LANES), lambda b, i: (b, 0, 0, 0))]
        out_specs, out_shape, kv_scratch = row_spec, y_shape, []
    return pl.pallas_call(
        kern,
        grid=(bsz // bb, seq // rb),
        in_specs=[pl.BlockSpec(memory_space=pltpu.SMEM),
                  row_spec, ada_spec, ada_spec, ada_spec,
                  _const_spec((D_MODEL, 2 * D_MODEL)), _const_spec((D_MODEL, D_MODEL)),
                  tab_spec, tab_spec,
                  _const_spec((1, D_MODEL)), _const_spec((1, D_MODEL))] + kv_specs,
        out_specs=out_specs,
        out_shape=out_shape,
        scratch_shapes=[pltpu.VMEM((rows, D_MODEL), _BF),
                        pltpu.VMEM((2, rows, HEAD_TILE), _F32),
                        pltpu.VMEM((D_MODEL, rows), _BF),
                        pltpu.VMEM((D_MODEL, rows), _F32),
                        pltpu.VMEM((D_MODEL, rows), _F32),
                        pltpu.VMEM((rows, HEAD_TILE), _BF),
                        pltpu.VMEM((rows, D_MODEL), _F32),
                        pltpu.VMEM((rows // LANES, PAIR_KEYS, LANES), _F32)]
        + kv_scratch,
        compiler_params=pltpu.CompilerParams(dimension_semantics=("arbitrary", "arbitrary"),
                                             vmem_limit_bytes=VMEM_LIMIT),
        name="attn_layer",
    )(sinks, x, shift, scale, gate, win, wout, *tables_t, vec(lng), vec(lnb), *kv_in)


def _run_trunk(x, ada_all, pos, h0, cache_k, cache_v, p):
    bsz, seq, _ = x.shape
    ada = lambda l: tuple(ada_all[l][:, k * D_MODEL:(k + 1) * D_MODEL] for k in range(3))
    states = []
    for la in range(N_A_LAYERS):
        x, h_last = _s5_layer(x, bsz, seq, ada(la), h0[la], p['win_a'][la], p['bs'][la], p['cs'][la], p['ks'][la],
                              p['ar'][la], p['ai'][la], p['ssm_d'][la], p['w_glu'][la], p['b_glu'][la],
                              p['w_out_a'][la], p['ln_g'][la], p['ln_b'][la],
                              batch_major_in=la == 0, batch_major_out=la == N_A_LAYERS - 1)
        states.append(_strips_to_state(h_last))
    banded = cache_k is None
    row_tables = _rope_tables(pos)
    if banded:
        kv, kv_maker, q_pos = None, (p['w_kv'], row_tables), pos
        bb, rb = 1, ATTN_ROWS
    else:
        k_new, v_new, k_bf = _shared_kv(x, p['w_kv'], row_tables)
        pad = LANES - seq
        flat = lambda c: c.reshape(bsz, c.shape[1], KV_WIDTH)
        k_bf = jnp.concatenate([flat(cache_k).astype(_BF), k_bf, jnp.zeros((bsz, pad, KV_WIDTH), _BF)], axis=1)
        v_all = jnp.concatenate([flat(cache_v), v_new, jnp.zeros((bsz, pad, KV_WIDTH), _F32)], axis=1)
        vt = v_all.reshape(bsz, PAIR_KEYS // LANES, LANES, KV_WIDTH).transpose(0, 1, 3, 2).astype(_BF)
        kv, kv_maker = (k_bf, vt), None
        x = jnp.pad(x, ((0, 0), (0, pad), (0, 0)))
        q_pos = jnp.tile(pos[0] + jnp.arange(LANES, dtype=jnp.int32), bsz)
        bb, rb = bsz, LANES
    cos, sin = _rope_angles(q_pos)
    tables_t = (cos.T, sin.T)
    for lb in range(N_B_LAYERS):
        layer = N_A_LAYERS + lb
        out = _attn_layer(x, ada(layer), p['win_b'][lb], p['w_out_b'][lb], p['attn_sinks'][lb], tables_t,
                          p['ln_g'][layer], p['ln_b'][layer], banded=banded, bb=bb, rb=rb, kv=kv, kv_maker=kv_maker)
        if kv is None:
            x, k_new, v_new, k_bf, vt = out
            kv = (k_bf, vt)
        else:
            x = out
    k4 = k_new.reshape(bsz, k_new.shape[1], N_KV_HEADS, HEAD_DIM)
    v4 = v_new.reshape(bsz, v_new.shape[1], N_KV_HEADS, HEAD_DIM)
    return x[:, :seq], jnp.stack(states), k4, v4


def kernel(x_prompt, x_sample, state_ssm, cache_k, cache_v, c_prompt, c_sample, w_ada, b_ada, ln_g, ln_b, w_in_a,
           ssm_a_re, ssm_a_im, ssm_b_re, ssm_b_im, ssm_c_re, ssm_c_im, ssm_d, ssm_log_dt, w_glu, b_glu, w_out_a,
           w_kv, w_in_b, attn_sinks, w_out_b):
    n_prompt, n_sample = x_prompt.shape[0], x_sample.shape[0]
    ada_all = _ada_params(jnp.concatenate([c_prompt, c_sample], axis=0), w_ada, b_ada)
    bs, cs, ks, ar, ai = _s5_matrices(
        _s5_prep(ssm_a_re, ssm_a_im, ssm_log_dt, ssm_b_re, ssm_b_im, ssm_c_re, ssm_c_im))
    p = dict(win_a=w_in_a.astype(_BF), bs=bs, cs=cs, ks=ks, ar=ar, ai=ai, ssm_d=ssm_d, w_glu=w_glu.astype(_BF),
             b_glu=b_glu, w_out_a=w_out_a.astype(_BF), ln_g=ln_g, ln_b=ln_b, w_kv=w_kv.astype(_BF),
             win_b=w_in_b.astype(_BF), w_out_b=w_out_b.astype(_BF), attn_sinks=attn_sinks)

    pos_prompt = jnp.arange(x_prompt.shape[1], dtype=jnp.int32)
    pos_sample = PAST_LEN + jnp.arange(x_sample.shape[1], dtype=jnp.int32)
    h0_prompt = jnp.zeros((N_A_LAYERS, N_STRIPS, n_prompt, STRIP_W), _F32)
    h0_sample = jnp.stack([_state_to_strips(state_ssm[la]) for la in range(N_A_LAYERS)])

    y_p, ssm_p, k_p, v_p = _run_trunk(x_prompt, ada_all[:, :n_prompt], pos_prompt, h0_prompt, None, None, p)
    y_s, ssm_s, k_s, v_s = _run_trunk(x_sample, ada_all[:, n_prompt:], pos_sample, h0_sample, cache_k, cache_v, p)
    rows = min(WINDOW, x_prompt.shape[1])
    return (y_p, y_s, ssm_p, k_p[:, -rows:], v_p[:, -rows:], ssm_s, k_s, v_s)
```

```python
import functools
import math

import jax
import jax.numpy as jnp
import numpy as np
from jax import lax
from jax.experimental import pallas as pl
from jax.experimental.pallas import tpu as pltpu

D_MODEL = 1024
DEPTH = 4
CHUNK = 64
N_A_LAYERS = 2
N_B_LAYERS = 2
SSM_GROUP_CH = 16
SSM_GROUPS = 64
SSM_STATE = 64
HEAD_DIM = 64
N_HEADS = 16
N_KV_HEADS = 4
Q_PER_KV = 4
KV_WIDTH = N_KV_HEADS * HEAD_DIM
WINDOW = 128
WINDOW_CHUNKS = WINDOW // CHUNK
PAST_LEN = 1024
ROPE_THETA = 10000.0
NEG_INF = -1e30
DEEPNORM_ALPHA = (2.0 * DEPTH) ** 0.25
LN_EPS = 1e-5

LANES = 128
STRIP_GROUPS = LANES // SSM_GROUP_CH
N_STRIPS = SSM_GROUPS // STRIP_GROUPS
STRIP_HALF = STRIP_GROUPS * SSM_STATE
STRIP_W = 2 * STRIP_HALF
SCAN_W = 256
S5_ROWS = 512
KEYS = (WINDOW_CHUNKS + 1) * CHUNK
PAIR_KEYS = KEYS + CHUNK
ATTN_ROWS = 512
HEAD_TILE = Q_PER_KV * HEAD_DIM
IN_PIECE = 512
KV_ROWS = 512
MASK_PASS = 3.0e38
LOG2_E = math.log2(math.e)
VMEM_LIMIT = 56 * 1024 * 1024

_BF = jnp.bfloat16
_F32 = jnp.float32


def _dot(a, b):
    return jnp.dot(a, b, preferred_element_type=_F32)


def _const_spec(shape):
    nd = len(shape)
    return pl.BlockSpec(shape, lambda *_: (0,) * nd, pipeline_mode=pl.Buffered(1))


def _layer_spec(shape, layer):
    nd = len(shape)
    return pl.BlockSpec((None,) + tuple(shape[1:]), lambda *_: (layer,) + (0,) * (nd - 1),
                        pipeline_mode=pl.Buffered(1))


def _layer_norm(r, g, b):
    mu = jnp.mean(r, axis=-1, keepdims=True)
    d = r - mu
    var = jnp.mean(d * d, axis=-1, keepdims=True)
    return d * lax.rsqrt(var + LN_EPS) * g + b


def _ada_kernel(c_ref, w_ref, b_ref, o_ref):
    c = c_ref[...]
    o_ref[0, 0] = _dot(jax.nn.silu(c).astype(_BF), w_ref[0].astype(_BF)) + b_ref[0]


def _ada_params(c_all, w_ada, b_ada):
    n = c_all.shape[0]
    return pl.pallas_call(
        _ada_kernel,
        grid=(DEPTH, 3),
        in_specs=[pl.BlockSpec((n, D_MODEL), lambda l, j: (0, 0)),
                  pl.BlockSpec((1, D_MODEL, D_MODEL), lambda l, j: (l, 0, j)),
                  pl.BlockSpec((1, 1, D_MODEL), lambda l, j: (l, 0, j))],
        out_specs=pl.BlockSpec((1, 1, n, D_MODEL), lambda l, j: (l, j, 0, 0)),
        out_shape=jax.ShapeDtypeStruct((DEPTH, 3, n, D_MODEL), _F32),
        name="ada_params",
    )(c_all, w_ada, b_ada.reshape(DEPTH, 1, 3 * D_MODEL))


def _cmul(x_re, x_im, y_re, y_im):
    return x_re * y_re - x_im * y_im, x_re * y_im + x_im * y_re


def _zoh_a(a_re, a_im, log_dt):
    dt = jnp.exp(log_dt)
    mag = jnp.exp(a_re * dt)
    return mag * jnp.cos(a_im * dt), mag * jnp.sin(a_im * dt)


def _s5_prep_kernel(are_ref, aim_ref, ldt_ref, bre_ref, bim_ref, aret_ref, aimt_ref, ldtt_ref, cre_ref, cim_ref,
                    a2r_ref, a2i_ref, bbr_ref, bbi_ref, abr_ref, abi_ref, car_ref, cai_ref, ca2r_ref, ca2i_ref):
    a_re, a_im = are_ref[0], aim_ref[0]
    ab_re, ab_im = _zoh_a(a_re, a_im, ldt_ref[0])
    nr, ni = ab_re - 1.0, ab_im
    den = a_re * a_re + a_im * a_im
    f_re = (nr * a_re + ni * a_im) / den
    f_im = (ni * a_re - nr * a_im) / den
    bb_re, bb_im = _cmul(f_re, f_im, bre_ref[0], bim_ref[0])
    a2r_ref[0], a2i_ref[0] = _cmul(ab_re, ab_im, ab_re, ab_im)
    bbr_ref[0], bbi_ref[0] = bb_re, bb_im
    abr_ref[0], abi_ref[0] = _cmul(ab_re, ab_im, bb_re, bb_im)
    at_re, at_im = _zoh_a(aret_ref[0], aimt_ref[0], ldtt_ref[0])
    ca_re, ca_im = _cmul(cre_ref[0], cim_ref[0], at_re, at_im)
    car_ref[0], cai_ref[0] = ca_re, ca_im
    ca2r_ref[0], ca2i_ref[0] = _cmul(ca_re, ca_im, at_re, at_im)


def _s5_lag_kernel(cr_ref, ci_ref, car_ref, cai_ref, bbr_ref, bbi_ref, k0_ref, k1_ref):
    def re_prod(x_re, x_im):
        dot = lambda x, y: jnp.einsum('gcp,gpd->gcd', x, y, precision=lax.Precision.HIGHEST,
                                      preferred_element_type=_F32)
        return dot(x_re, bbr_ref[0]) - dot(x_im, bbi_ref[0])
    k0_ref[0] = re_prod(cr_ref[0], ci_ref[0])
    k1_ref[0] = re_prod(car_ref[0], cai_ref[0])


def _s5_prep(a_re, a_im, log_dt, b_re, b_im, c_re, c_im):
    nl, w = N_A_LAYERS, SSM_STATE * SSM_GROUP_CH
    flat = lambda a: a.reshape(nl, SSM_GROUPS, w)
    rep = lambda a: jnp.repeat(a, SSM_GROUP_CH, axis=-1)
    til = lambda a: jnp.tile(a, (1, 1, SSM_GROUP_CH))
    ldt = log_dt[:, :, None]
    spec = pl.BlockSpec((1, SSM_GROUPS, w), lambda l: (l, 0, 0))
    shp = jax.ShapeDtypeStruct((nl, SSM_GROUPS, w), _F32)
    a2_re, a2_im, bb_re, bb_im, abb_re, abb_im, ca_re, ca_im, ca2_re, ca2_im = pl.pallas_call(
        _s5_prep_kernel, grid=(nl,), in_specs=[spec] * 10, out_specs=[spec] * 10, out_shape=[shp] * 10, name="s5_prep",
    )(rep(a_re), rep(a_im), rep(jnp.broadcast_to(ldt, a_re.shape)), flat(b_re), flat(b_im),
      til(a_re), til(a_im), til(jnp.broadcast_to(ldt, a_re.shape)), flat(c_re), flat(c_im))
    pc = (nl, SSM_GROUPS, SSM_STATE, SSM_GROUP_CH)
    cp = (nl, SSM_GROUPS, SSM_GROUP_CH, SSM_STATE)
    cp_spec = pl.BlockSpec((1,) + cp[1:], lambda l: (l, 0, 0, 0))
    pc_spec = pl.BlockSpec((1,) + pc[1:], lambda l: (l, 0, 0, 0))
    kk = (nl, SSM_GROUPS, SSM_GROUP_CH, SSM_GROUP_CH)
    kk_spec = pl.BlockSpec((1,) + kk[1:], lambda l: (l, 0, 0, 0))
    k0, k1 = pl.pallas_call(
        _s5_lag_kernel, grid=(nl,), in_specs=[cp_spec] * 4 + [pc_spec] * 2, out_specs=[kk_spec] * 2,
        out_shape=[jax.ShapeDtypeStruct(kk, _F32)] * 2, name="s5_lag",
    )(c_re, c_im, ca_re.reshape(cp), ca_im.reshape(cp), bb_re.reshape(pc), bb_im.reshape(pc))
    return dict(a2_re=a2_re.reshape(pc)[..., 0], a2_im=a2_im.reshape(pc)[..., 0],
                bb_re=bb_re.reshape(pc), bb_im=bb_im.reshape(pc), abb_re=abb_re.reshape(pc), abb_im=abb_im.reshape(pc),
                ca_re=ca_re.reshape(cp), ca_im=ca_im.reshape(cp), ca2_re=ca2_re.reshape(cp), ca2_im=ca2_im.reshape(cp),
                k0=k0, k1=k1)


def _s5_matrices(q):
    same = jnp.eye(STRIP_GROUPS, dtype=bool)
    nl = N_A_LAYERS
    zero = jnp.zeros((), _BF)

    def expand_b(re, im):
        bb = jnp.stack([re, im], axis=1).astype(_BF)
        bb = bb.reshape(nl, 2, N_STRIPS, STRIP_GROUPS, SSM_STATE, SSM_GROUP_CH).transpose(0, 2, 3, 5, 1, 4)
        placed = jnp.where(same[None, None, :, None, None, :, None], bb[:, :, :, :, :, None, :], zero)
        return placed.reshape(nl, N_STRIPS, LANES, STRIP_W)

    def project_c(re, im):
        cc = jnp.stack([re, -im], axis=1).astype(_BF)
        cc = cc.reshape(nl, 2, N_STRIPS, STRIP_GROUPS, SSM_GROUP_CH, SSM_STATE).transpose(0, 2, 1, 5, 3, 4)
        placed = jnp.where(same[None, None, None, :, None, :, None], cc[:, :, :, None, :, :, :], zero)
        return placed.reshape(nl, N_STRIPS, STRIP_W, LANES)

    def lag(k):
        kt = k.astype(_BF).transpose(0, 1, 3, 2).reshape(nl, N_STRIPS, STRIP_GROUPS, SSM_GROUP_CH, SSM_GROUP_CH)
        placed = jnp.where(same[None, None, :, None, :, None], kt[:, :, :, :, None, :], zero)
        return placed.reshape(nl, N_STRIPS, LANES, LANES)

    bs2 = jnp.concatenate([expand_b(q['abb_re'], q['abb_im']), expand_b(q['bb_re'], q['bb_im'])], axis=2)
    cs2 = jnp.concatenate([project_c(q['ca_re'], q['ca_im']), project_c(q['ca2_re'], q['ca2_im'])], axis=3)
    k0, k1 = lag(q['k0']), lag(q['k1'])
    ks = jnp.concatenate([jnp.concatenate([k0, k1], axis=3),
                          jnp.concatenate([jnp.zeros_like(k0), k0], axis=3)], axis=2)
    a2r = q['a2_re'].reshape(nl, N_STRIPS, 1, STRIP_HALF)
    a2i = q['a2_im'].reshape(nl, N_STRIPS, 1, STRIP_HALF)
    return bs2, cs2, ks, a2r, a2i


def _state_to_strips(state):
    nb = state.shape[0]
    s = state.reshape(nb, N_STRIPS, STRIP_GROUPS, SSM_STATE, 2)
    return s.transpose(1, 0, 4, 2, 3).reshape(N_STRIPS, nb, STRIP_W)


def _strips_to_state(h):
    nb = h.shape[1]
    s = h.reshape(N_STRIPS, nb, 2, STRIP_GROUPS, SSM_STATE)
    return s.transpose(1, 0, 3, 4, 2).reshape(nb, SSM_GROUPS, SSM_STATE, 2)


def _s5_layer_kernel(x_ref, ada_ref, h0_ref, win_ref, bs_ref, cs_ref, ks_ref, ar_ref, ai_ref,
                     d_ref, wglu_ref, bglu_ref, wout_ref, lng_ref, lnb_ref,
                     y_ref, hout_ref,
                     h_scr, u_scr, z_scr, bu_scr, yy_scr, xt_scr, *, nb, steps, row0, batch_major_in, batch_major_out):
    rows = nb * steps
    shift, scale, gate = (ada_ref[k, row0:row0 + nb, :] for k in range(3))
    i = pl.program_id(0)

    @pl.when(i == 0)
    def _():
        h_scr[...] = h0_ref[...]

    lane_tiles = D_MODEL // LANES
    if batch_major_in:
        for b in range(nb):
            for c in range(lane_tiles):
                xt_scr[c, pl.ds(b, steps, stride=nb), :] = x_ref[b, :, c * LANES:(c + 1) * LANES]
        x_rows = jnp.concatenate([xt_scr[c] for c in range(lane_tiles)], axis=1)
    else:
        x_rows = x_ref[...]
    x3 = x_rows.reshape(steps, nb, D_MODEL)
    hmod = (x3 * (1.0 + scale)[None] + shift[None]).reshape(rows, D_MODEL)
    uz = _dot(hmod.astype(_BF), win_ref[...])
    pairs = steps // 2
    prow = pairs * nb
    for j in range(N_STRIPS):
        u_scr[j] = uz[:, j * LANES:(j + 1) * LANES].reshape(pairs, 2, nb, LANES)
    z_scr[...] = uz[:, D_MODEL:]

    def pair_inputs(j):
        return jnp.concatenate([u_scr[j, :, 0].reshape(prow, LANES), u_scr[j, :, 1].reshape(prow, LANES)], axis=1)

    def expand(j):
        bu_scr[j % 2] = _dot(pair_inputs(j).astype(_BF), bs_ref[j])

    def scan(j):
        buf = j % 2
        a_re_row, a_im_row = ar_ref[j], ai_ref[j]
        for s in range(STRIP_HALF // SCAN_W):
            re = slice(s * SCAN_W, (s + 1) * SCAN_W)
            im = slice(STRIP_HALF + s * SCAN_W, STRIP_HALF + (s + 1) * SCAN_W)
            a_re = jnp.broadcast_to(a_re_row[:, re], (nb, SCAN_W))
            a_im = jnp.broadcast_to(a_im_row[:, re], (nb, SCAN_W))
            h_re, h_im = h_scr[j, :, re], h_scr[j, :, im]
            for t in range(pairs):
                now = slice(t * nb, (t + 1) * nb)
                n_re = a_re * h_re - a_im * h_im + bu_scr[buf, now, re]
                n_im = a_re * h_im + a_im * h_re + bu_scr[buf, now, im]
                bu_scr[buf, now, re] = h_re
                bu_scr[buf, now, im] = h_im
                h_re, h_im = n_re, n_im
            h_scr[j, :, re] = h_re
            h_scr[j, :, im] = h_im

    def project(j):
        yy = _dot(bu_scr[j % 2].astype(_BF), cs_ref[j]) + _dot(pair_inputs(j).astype(_BF), ks_ref[j])
        for par in range(2):
            y_par = yy[:, par * LANES:(par + 1) * LANES].reshape(pairs, nb, LANES) + d_ref[j] * u_scr[j, :, par]
            yy_scr[j, :, par] = y_par

    expand(0)
    for j in range(N_STRIPS):
        if j + 1 < N_STRIPS:
            expand(j + 1)
        scan(j)
        project(j)

    y = jnp.concatenate([yy_scr[j].reshape(rows, LANES) for j in range(N_STRIPS)], axis=1)
    g = jax.nn.gelu(y)
    y = g * jax.nn.sigmoid(_dot(g.astype(_BF), wglu_ref[...]) + bglu_ref[...])
    y = y * jax.nn.silu(z_scr[...])
    out = _dot(y.astype(_BF), wout_ref[...]).reshape(steps, nb, D_MODEL)
    r = (DEEPNORM_ALPHA * x3 + gate[None] * out).reshape(rows, D_MODEL)
    y = _layer_norm(r, lng_ref[...], lnb_ref[...])
    if batch_major_out:
        for c in range(lane_tiles):
            xt_scr[c] = y[:, c * LANES:(c + 1) * LANES]
        for b in range(nb):
            for c in range(lane_tiles):
                y_ref[b, :, c * LANES:(c + 1) * LANES] = xt_scr[c, pl.ds(b, steps, stride=nb), :]
    else:
        y_ref[...] = y

    @pl.when(i == pl.num_programs(0) - 1)
    def _():
        hout_ref[...] = h_scr[...]


def _s5_layer(x, nb, seq, ada_all, row0, h0, la, p, batch_major_in, batch_major_out):
    total = seq * nb
    rows = min(S5_ROWS, total)
    steps = rows // nb
    tb_spec = pl.BlockSpec((rows, D_MODEL), lambda i: (i, 0))
    bt_spec = pl.BlockSpec((nb, steps, D_MODEL), lambda i: (0, i, 0))
    tb_shape = jax.ShapeDtypeStruct((total, D_MODEL), _F32)
    bt_shape = jax.ShapeDtypeStruct((nb, seq, D_MODEL), _F32)
    kern = functools.partial(_s5_layer_kernel, nb=nb, steps=steps, row0=row0, batch_major_in=batch_major_in,
                             batch_major_out=batch_major_out)
    weights = [p[k] for k in ('win_a', 'bs', 'cs', 'ks', 'ar', 'ai', 'ssm_d', 'w_glu', 'b_glu', 'w_out_a', 'ln_g', 'ln_b')]
    return pl.pallas_call(
        kern,
        grid=(total // rows,),
        in_specs=[bt_spec if batch_major_in else tb_spec,
                  _layer_spec(ada_all.shape, la),
                  _layer_spec(h0.shape, la)] + [_layer_spec(w.shape, la) for w in weights],
        out_specs=[bt_spec if batch_major_out else tb_spec,
                   pl.BlockSpec((N_STRIPS, nb, STRIP_W), lambda i: (0, 0, 0))],
        out_shape=[bt_shape if batch_major_out else tb_shape,
                   jax.ShapeDtypeStruct((N_STRIPS, nb, STRIP_W), _F32)],
        scratch_shapes=[pltpu.VMEM((N_STRIPS, nb, STRIP_W), _F32),
                        pltpu.VMEM((N_STRIPS, steps // 2, 2, nb, LANES), _F32),
                        pltpu.VMEM((rows, D_MODEL), _F32),
                        pltpu.VMEM((2, rows // 2, STRIP_W), _F32),
                        pltpu.VMEM((N_STRIPS, steps // 2, 2, nb, LANES), _F32),
                        pltpu.VMEM((D_MODEL // LANES, rows, LANES), _F32)],
        compiler_params=pltpu.CompilerParams(dimension_semantics=("arbitrary",), vmem_limit_bytes=VMEM_LIMIT),
        name="s5_layer",
    )(x, ada_all, h0, *weights)


def _rope_angles(pos):
    half = HEAD_DIM // 2
    inv_freq = np.power(ROPE_THETA, -np.arange(half, dtype=np.float64) / half)
    ang = pos.astype(np.float64)[:, None] * inv_freq[None, :]
    return np.cos(ang).astype(np.float32), np.sin(ang).astype(np.float32)


def _rope_tables(pos):
    cos, sin = _rope_angles(pos)
    zero = np.zeros_like(sin)
    cos_t = np.tile(cos, (1, LANES // (HEAD_DIM // 2)))
    sin_lo = np.tile(np.concatenate([-sin, zero], axis=1), (1, LANES // HEAD_DIM))
    sin_hi = np.tile(np.concatenate([zero, sin], axis=1), (1, LANES // HEAD_DIM))
    return jnp.asarray(cos_t), jnp.asarray(sin_lo), jnp.asarray(sin_hi)


def _rope_tile(x, cos_t, sin_lo, sin_hi):
    half = HEAD_DIM // 2
    return x * cos_t + pltpu.roll(x, LANES - half, 1) * sin_lo + pltpu.roll(x, half, 1) * sin_hi


def _kv_kernel(x_ref, w_ref, cos_ref, slo_ref, shi_ref, k_ref, v_ref, kbf_ref):
    kv = _dot(x_ref[0].astype(_BF), w_ref[...])
    cos_t, sin_lo, sin_hi = cos_ref[...], slo_ref[...], shi_ref[...]
    for t in range(KV_WIDTH // LANES):
        tile = _rope_tile(kv[:, t * LANES:(t + 1) * LANES], cos_t, sin_lo, sin_hi)
        k_ref[0, :, t * LANES:(t + 1) * LANES] = tile
        kbf_ref[0, :, t * LANES:(t + 1) * LANES] = tile.astype(_BF)
    v_ref[0] = kv[:, KV_WIDTH:]


def _shared_kv(x, w_kv, tables):
    bsz, seq, _ = x.shape
    rows = min(KV_ROWS, seq)
    tab_spec = pl.BlockSpec((rows, LANES), lambda b, i: (i, 0))
    kv_spec = pl.BlockSpec((1, rows, KV_WIDTH), lambda b, i: (b, i, 0))
    out_specs = [kv_spec] * 3
    out_shape = ([jax.ShapeDtypeStruct((bsz, seq, KV_WIDTH), _F32)] * 2
                 + [jax.ShapeDtypeStruct((bsz, seq, KV_WIDTH), _BF)])
    return pl.pallas_call(
        _kv_kernel,
        grid=(bsz, seq // rows),
        in_specs=[pl.BlockSpec((1, rows, D_MODEL), lambda b, i: (b, i, 0)),
                  _const_spec((D_MODEL, 2 * KV_WIDTH)), tab_spec, tab_spec, tab_spec],
        out_specs=out_specs, out_shape=out_shape,
        compiler_params=pltpu.CompilerParams(dimension_semantics=("arbitrary", "arbitrary"),
                                             vmem_limit_bytes=VMEM_LIMIT),
        name="shared_kv",
    )(x, w_kv, *tables)


def _reduce_rows(x, pair_op, final_op):
    sub = 8
    parts = [x[i * sub:(i + 1) * sub] for i in range(x.shape[0] // sub)]
    while len(parts) > 1:
        parts = [pair_op(parts[i], parts[i + 1]) for i in range(0, len(parts), 2)]
    return final_op(parts[0], axis=0, keepdims=True)


def _attn_layer_kernel(sinks_ref, x_ref, ada_ref, win_ref, wout_ref, cos_ref, sin_ref,
                       lng_ref, lnb_ref, *refs, bb, rb, row0, layer_b, banded, make_kv):
    rows = bb * rb
    n_pairs = rows // LANES
    pairs_per_stream = rb // LANES
    if make_kv:
        (wkv_ref, cosr_ref, slo_ref, shi_ref, y_ref, kcache_ref, vcache_ref, kbf_ref, vtout_ref,
         hm_scr, blk_scr, qt_scr, gt_scr, ot_scr, ob_scr, acc_scr, cap_scr, k_src, vt_src) = refs
        step = pl.program_id(1)
        kv_new = _dot(x_ref[0].astype(_BF), wkv_ref[...])
        first_row = pl.multiple_of(step * rb, rb)
        for t in range(KV_WIDTH // LANES):
            tile = _rope_tile(kv_new[:, t * LANES:(t + 1) * LANES], cosr_ref[...], slo_ref[...], shi_ref[...])
            kcache_ref[0, :, t * LANES:(t + 1) * LANES] = tile[rb - WINDOW:]
            kbf_ref[0, :, t * LANES:(t + 1) * LANES] = tile.astype(_BF)
            k_src[0, pl.ds(first_row, rb), t * LANES:(t + 1) * LANES] = tile.astype(_BF)
        v_new = kv_new[:, KV_WIDTH:]
        vcache_ref[0] = v_new[rb - WINDOW:]
        for blk in range(rb // LANES):
            vt_blk = v_new[blk * LANES:(blk + 1) * LANES].T.astype(_BF)
            vtout_ref[0, blk] = vt_blk
            vt_src[0, step * (rb // LANES) + blk] = vt_blk
    else:
        (k_src, vt_src, y_ref, hm_scr, blk_scr, qt_scr, gt_scr, ot_scr, ob_scr, acc_scr, cap_scr) = refs

    def ada_row(k, b):
        if banded:
            return ada_ref[k, pl.ds(row0 + pl.program_id(0) * bb + b, 1), :]
        return ada_ref[k, row0 + b:row0 + b + 1, :]

    for b in range(bb):
        hm_scr[b * rb:(b + 1) * rb] = (x_ref[b] * (1.0 + ada_row(1, b)) + ada_row(0, b)).astype(_BF)
    cos_t, sin_t = cos_ref[...], sin_ref[...]
    half = HEAD_DIM // 2
    qk_scale = HEAD_DIM ** -0.5 * LOG2_E
    kt = IN_PIECE

    def in_piece(c, k):
        part = _dot(hm_scr[:, k * kt:(k + 1) * kt], win_ref[k * kt:(k + 1) * kt, c * HEAD_TILE:(c + 1) * HEAD_TILE])
        if k == 0:
            blk_scr[c // N_KV_HEADS] = part
        else:
            blk_scr[c // N_KV_HEADS] += part

    def in_finish(c):
        t = blk_scr[c // N_KV_HEADS].T
        if c < N_KV_HEADS:
            for j in range(Q_PER_KV):
                lo = c * HEAD_TILE + j * HEAD_DIM
                x1, x2 = t[j * HEAD_DIM:j * HEAD_DIM + half], t[j * HEAD_DIM + half:(j + 1) * HEAD_DIM]
                qt_scr[lo:lo + half] = ((x1 * cos_t - x2 * sin_t) * qk_scale).astype(_BF)
                qt_scr[lo + half:lo + HEAD_DIM] = ((x2 * cos_t + x1 * sin_t) * qk_scale).astype(_BF)
        else:
            lo = (c - N_KV_HEADS) * HEAD_TILE
            gt_scr[lo:lo + HEAD_TILE] = jax.nn.silu(t)

    def out_prepare(kv):
        ob_scr[...] = ot_scr[kv * HEAD_TILE:(kv + 1) * HEAD_TILE].T.astype(_BF)

    def out_piece(kv, n):
        part = _dot(ob_scr[...], wout_ref[kv * HEAD_TILE:(kv + 1) * HEAD_TILE, n * HEAD_TILE:(n + 1) * HEAD_TILE])
        if kv == 0:
            acc_scr[:, n * HEAD_TILE:(n + 1) * HEAD_TILE] = part
        else:
            acc_scr[:, n * HEAD_TILE:(n + 1) * HEAD_TILE] += part

    key_blk = lax.broadcasted_iota(jnp.int32, (PAIR_KEYS, LANES), 0) // CHUNK
    qry_blk = lax.broadcasted_iota(jnp.int32, (PAIR_KEYS, LANES), 1) // CHUNK
    no_q = jnp.zeros((HEAD_DIM, LANES), _BF)
    pairs = []
    for a in range(n_pairs):
        if banded:
            qc0 = pl.program_id(1) * (rb // CHUNK) + 2 * (a % pairs_per_stream)
            kc0 = jnp.maximum(qc0 - WINDOW_CHUNKS, 0)
            delta0 = kc0 - qc0
            kstart = pl.multiple_of(kc0 * CHUNK, LANES)
            vblk = kc0 // 2
        else:
            delta0, kstart, vblk = -WINDOW_CHUNKS, 0, 0
        rel = delta0 + key_blk - qry_blk
        cap_scr[a] = jnp.where((rel >= -WINDOW_CHUNKS) & (rel <= 0), MASK_PASS, NEG_INF)
        pairs.append((a // pairs_per_stream, kstart, vblk))

    def unit_heads(kv, gp):
        return (kv * Q_PER_KV + 2 * gp, kv * Q_PER_KV + 2 * gp + 1)

    def scores(a, kv, gp):
        lanes = slice(a * LANES, (a + 1) * LANES)
        stream, kstart, _ = pairs[a]
        k_tile = k_src[stream, pl.ds(kstart, PAIR_KEYS), (kv // 2) * LANES:(kv // 2 + 1) * LANES]
        w_parts = []
        for h in unit_heads(kv, gp):
            q_h = qt_scr[h * HEAD_DIM:(h + 1) * HEAD_DIM, lanes]
            w_parts.append(jnp.concatenate([q_h, no_q] if kv % 2 == 0 else [no_q, q_h], axis=0))
        return _dot(k_tile, jnp.concatenate(w_parts, axis=1))

    def softmax_values(a, kv, gp, s):
        e_parts, inv_den = [], []
        for u, h in enumerate(unit_heads(kv, gp)):
            s_h = jnp.minimum(s[:, u * LANES:(u + 1) * LANES], cap_scr[a])
            sink = sinks_ref[layer_b, h] * LOG2_E
            m = jnp.maximum(_reduce_rows(s_h, jnp.maximum, jnp.max), sink)
            e = jnp.exp2(s_h - m)
            inv_den.append(1.0 / (_reduce_rows(e, jnp.add, jnp.sum) + jnp.exp2(sink - m)))
            e_parts.append(e.astype(_BF))
        stream, _, vblk = pairs[a]
        vt_kv = jnp.concatenate([vt_src[stream, vblk, kv * HEAD_DIM:(kv + 1) * HEAD_DIM, :],
                                 vt_src[stream, vblk + 1, kv * HEAD_DIM:(kv + 1) * HEAD_DIM, :]], axis=1)
        return _dot(vt_kv, jnp.concatenate(e_parts, axis=1)), inv_den

    def finish(a, kv, gp, ot, inv_den):
        lanes = slice(a * LANES, (a + 1) * LANES)
        for u, h in enumerate(unit_heads(kv, gp)):
            rows_h = slice(h * HEAD_DIM, (h + 1) * HEAD_DIM)
            ot_scr[rows_h, lanes] = ot[:, u * LANES:(u + 1) * LANES] * inv_den[u] * gt_scr[rows_h, lanes]

    n_k = D_MODEL // kt
    for c in (0, N_KV_HEADS):
        for k in range(n_k):
            in_piece(c, k)
        in_finish(c)
    slots = []
    for kv in range(N_KV_HEADS):
        units = [(a, kv, gp) for a in range(n_pairs) for gp in range(Q_PER_KV // 2)]
        ahead, behind = [], []
        if kv + 1 < N_KV_HEADS:
            for c in (kv + 1, kv + 1 + N_KV_HEADS):
                ahead += [functools.partial(in_piece, c, k) for k in range(n_k)] + [functools.partial(in_finish, c)]
        if kv > 0:
            behind = [functools.partial(out_prepare, kv - 1)] + [
                functools.partial(out_piece, kv - 1, n) for n in range(D_MODEL // HEAD_TILE)]
        work = [[] for _ in units]
        for m, item in enumerate(ahead):
            work[m * len(units) // len(ahead)].append(item)
        for m, item in enumerate(behind):
            work[1 + m * (len(units) - 1) // len(behind)].append(item)
        slots += list(zip(units, work))
    s_next, unfinished = scores(*slots[0][0]), None
    for i, (unit, work) in enumerate(slots):
        for item in work:
            item()
        s_cur = s_next
        if i + 1 < len(slots):
            s_next = scores(*slots[i + 1][0])
        ot, inv_den = softmax_values(*unit, s_cur)
        if unfinished is not None:
            finish(*unfinished)
        unfinished = (*unit, ot, inv_den)
    finish(*unfinished)
    out_prepare(N_KV_HEADS - 1)
    for n in range(D_MODEL // HEAD_TILE):
        out_piece(N_KV_HEADS - 1, n)

    for b in range(bb):
        r = DEEPNORM_ALPHA * x_ref[b] + ada_row(2, b) * acc_scr[b * rb:(b + 1) * rb]
        y_ref[b] = _layer_norm(r, lng_ref[...], lnb_ref[...])


def _attn_layer(x, ada_all, row0, lb, p, tables_t, banded, bb, rb, kv=None, kv_maker=None):
    bsz, seq, _ = x.shape
    rows = bb * rb
    layer = N_A_LAYERS + lb
    row_spec = pl.BlockSpec((bb, rb, D_MODEL), lambda b, i: (b, i, 0))
    tab_spec = pl.BlockSpec((HEAD_DIM // 2, rows), lambda b, i: (0, i))
    y_shape = jax.ShapeDtypeStruct((bsz, seq, D_MODEL), _F32)
    kern = functools.partial(_attn_layer_kernel, bb=bb, rb=rb, row0=row0, layer_b=lb, banded=banded,
                             make_kv=kv is None)
    if kv is None:
        assert bb == 1
        w_kv, row_tables = kv_maker
        rtab_spec = pl.BlockSpec((rb, LANES), lambda b, i: (i, 0))
        kv_in, kv_specs = (w_kv, *row_tables), [_const_spec((D_MODEL, 2 * KV_WIDTH))] + [rtab_spec] * 3
        cache_spec = pl.BlockSpec((1, WINDOW, KV_WIDTH), lambda b, i: (b, 0, 0))
        out_specs = [row_spec, cache_spec, cache_spec,
                     pl.BlockSpec((1, rb, KV_WIDTH), lambda b, i: (b, i, 0)),
                     pl.BlockSpec((1, rb // LANES, KV_WIDTH, LANES), lambda b, i: (b, i, 0, 0))]
        out_shape = [y_shape] + [jax.ShapeDtypeStruct((bsz, WINDOW, KV_WIDTH), _F32)] * 2 + [
            jax.ShapeDtypeStruct((bsz, seq, KV_WIDTH), _BF),
            jax.ShapeDtypeStruct((bsz, seq // LANES, KV_WIDTH, LANES), _BF)]
        kv_scratch = [pltpu.VMEM((1, seq, KV_WIDTH), _BF), pltpu.VMEM((1, seq // LANES, KV_WIDTH, LANES), _BF)]
    else:
        keys = kv[0].shape[1]
        kv_in, kv_specs = kv, [pl.BlockSpec((bb, keys, KV_WIDTH), lambda b, i: (b, 0, 0)),
                               pl.BlockSpec((bb, keys // LANES, KV_WIDTH, LANES), lambda b, i: (b, 0, 0, 0))]
        out_specs, out_shape, kv_scratch = row_spec, y_shape, []
    return pl.pallas_call(
        kern,
        grid=(bsz // bb, seq // rb),
        in_specs=[pl.BlockSpec(memory_space=pltpu.SMEM),
                  row_spec, _layer_spec(ada_all.shape, layer),
                  _layer_spec(p['win_b'].shape, lb), _layer_spec(p['w_out_b'].shape, lb),
                  tab_spec, tab_spec,
                  _layer_spec(p['ln_g'].shape, layer), _layer_spec(p['ln_b'].shape, layer)] + kv_specs,
        out_specs=out_specs,
        out_shape=out_shape,
        scratch_shapes=[pltpu.VMEM((rows, D_MODEL), _BF),
                        pltpu.VMEM((2, rows, HEAD_TILE), _F32),
                        pltpu.VMEM((D_MODEL, rows), _BF),
                        pltpu.VMEM((D_MODEL, rows), _F32),
                        pltpu.VMEM((D_MODEL, rows), _F32),
                        pltpu.VMEM((rows, HEAD_TILE), _BF),
                        pltpu.VMEM((rows, D_MODEL), _F32),
                        pltpu.VMEM((rows // LANES, PAIR_KEYS, LANES), _F32)]
        + kv_scratch,
        compiler_params=pltpu.CompilerParams(dimension_semantics=("arbitrary", "arbitrary"),
                                             vmem_limit_bytes=VMEM_LIMIT),
        name="attn_layer",
    )(p['attn_sinks'], x, ada_all, p['win_b'], p['w_out_b'], *tables_t, p['ln_g'], p['ln_b'], *kv_in)


def _run_trunk(x, ada_all, row0, pos, h0, cache_k, cache_v, p):
    bsz, seq, _ = x.shape
    states = []
    for la in range(N_A_LAYERS):
        x, h_last = _s5_layer(x, bsz, seq, ada_all, row0, h0, la, p,
                              batch_major_in=la == 0, batch_major_out=la == N_A_LAYERS - 1)
        states.append(_strips_to_state(h_last))
    banded = cache_k is None
    row_tables = _rope_tables(pos)
    if banded:
        kv, kv_maker, q_pos = None, (p['w_kv'], row_tables), pos
        bb, rb = 1, ATTN_ROWS
    else:
        k_new, v_new, k_bf = _shared_kv(x, p['w_kv'], row_tables)
        pad = LANES - seq
        flat = lambda c: c.reshape(bsz, c.shape[1], KV_WIDTH)
        k_bf = jnp.concatenate([flat(cache_k).astype(_BF), k_bf, jnp.zeros((bsz, pad, KV_WIDTH), _BF)], axis=1)
        v_all = jnp.concatenate([flat(cache_v), v_new, jnp.zeros((bsz, pad, KV_WIDTH), _F32)], axis=1)
        vt = v_all.reshape(bsz, PAIR_KEYS // LANES, LANES, KV_WIDTH).transpose(0, 1, 3, 2).astype(_BF)
        kv, kv_maker = (k_bf, vt), None
        x = jnp.pad(x, ((0, 0), (0, pad), (0, 0)))
        q_pos = np.tile(pos[0] + np.arange(LANES), bsz)
        bb, rb = bsz, LANES
    cos, sin = _rope_angles(q_pos)
    tables_t = (jnp.asarray(cos.T), jnp.asarray(sin.T))
    for lb in range(N_B_LAYERS):
        out = _attn_layer(x, ada_all, row0, lb, p, tables_t, banded=banded, bb=bb, rb=rb, kv=kv, kv_maker=kv_maker)
        if kv is None:
            x, k_new, v_new, k_bf, vt = out
            kv = (k_bf, vt)
        else:
            x = out
    k4 = k_new.reshape(bsz, k_new.shape[1], N_KV_HEADS, HEAD_DIM)
    v4 = v_new.reshape(bsz, v_new.shape[1], N_KV_HEADS, HEAD_DIM)
    return x[:, :seq], jnp.stack(states), k4, v4


def kernel(x_prompt, x_sample, state_ssm, cache_k, cache_v, c_prompt, c_sample, w_ada, b_ada, ln_g, ln_b, w_in_a,
           ssm_a_re, ssm_a_im, ssm_b_re, ssm_b_im, ssm_c_re, ssm_c_im, ssm_d, ssm_log_dt, w_glu, b_glu, w_out_a,
           w_kv, w_in_b, attn_sinks, w_out_b):
    n_prompt, n_sample = x_prompt.shape[0], x_sample.shape[0]
    ada_all = _ada_params(jnp.concatenate([c_prompt, c_sample], axis=0), w_ada, b_ada)
    bs, cs, ks, ar, ai = _s5_matrices(
        _s5_prep(ssm_a_re, ssm_a_im, ssm_log_dt, ssm_b_re, ssm_b_im, ssm_c_re, ssm_c_im))
    p = dict(win_a=w_in_a.astype(_BF), bs=bs, cs=cs, ks=ks, ar=ar, ai=ai,
             ssm_d=ssm_d.reshape(N_A_LAYERS, N_STRIPS, 1, LANES), w_glu=w_glu.astype(_BF),
             b_glu=b_glu.reshape(N_A_LAYERS, 1, D_MODEL), w_out_a=w_out_a.astype(_BF),
             ln_g=ln_g.reshape(DEPTH, 1, D_MODEL), ln_b=ln_b.reshape(DEPTH, 1, D_MODEL), w_kv=w_kv.astype(_BF),
             win_b=w_in_b.astype(_BF), w_out_b=w_out_b.astype(_BF), attn_sinks=attn_sinks)

    pos_prompt = np.arange(x_prompt.shape[1])
    pos_sample = PAST_LEN + np.arange(x_sample.shape[1])
    h0_prompt = jnp.zeros((N_A_LAYERS, N_STRIPS, n_prompt, STRIP_W), _F32)
    h0_sample = jnp.stack([_state_to_strips(state_ssm[la]) for la in range(N_A_LAYERS)])

    y_p, ssm_p, k_p, v_p = _run_trunk(x_prompt, ada_all, 0, pos_prompt, h0_prompt, None, None, p)
    y_s, ssm_s, k_s, v_s = _run_trunk(x_sample, ada_all, n_prompt, pos_sample, h0_sample, cache_k, cache_v, p)
    rows = min(WINDOW, x_prompt.shape[1])
    return (y_p, y_s, ssm_p, k_p[:, -rows:], v_p[:, -rows:], ssm_s, k_s, v_s)
```

```python
import functools
import math

import jax
import jax.numpy as jnp
import numpy as np
from jax import lax
from jax.experimental import pallas as pl
from jax.experimental.pallas import tpu as pltpu

D_MODEL = 1024
DEPTH = 4
CHUNK = 64
N_A_LAYERS = 2
N_B_LAYERS = 2
SSM_GROUP_CH = 16
SSM_GROUPS = 64
SSM_STATE = 64
HEAD_DIM = 64
N_HEADS = 16
N_KV_HEADS = 4
Q_PER_KV = 4
KV_WIDTH = N_KV_HEADS * HEAD_DIM
WINDOW = 128
WINDOW_CHUNKS = WINDOW // CHUNK
PAST_LEN = 1024
ROPE_THETA = 10000.0
NEG_INF = -1e30
DEEPNORM_ALPHA = (2.0 * DEPTH) ** 0.25
LN_EPS = 1e-5

LANES = 128
STRIP_GROUPS = LANES // SSM_GROUP_CH
N_STRIPS = SSM_GROUPS // STRIP_GROUPS
STRIP_HALF = STRIP_GROUPS * SSM_STATE
STRIP_W = 2 * STRIP_HALF
SCAN_W = 256
S5_ROWS = 512
KEYS = (WINDOW_CHUNKS + 1) * CHUNK
PAIR_KEYS = KEYS + CHUNK
ATTN_ROWS = 512
HEAD_TILE = Q_PER_KV * HEAD_DIM
IN_PIECE = 512
KV_ROWS = 512
MASK_PASS = 3.0e38
LOG2_E = math.log2(math.e)
VMEM_LIMIT = 56 * 1024 * 1024

_BF = jnp.bfloat16
_F32 = jnp.float32


def _dot(a, b):
    return jnp.dot(a, b, preferred_element_type=_F32)


def _const_spec(shape):
    nd = len(shape)
    return pl.BlockSpec(shape, lambda *_: (0,) * nd, pipeline_mode=pl.Buffered(1))


def _layer_spec(shape, layer):
    nd = len(shape)
    return pl.BlockSpec((None,) + tuple(shape[1:]), lambda *_: (layer,) + (0,) * (nd - 1),
                        pipeline_mode=pl.Buffered(1))


def _layer_norm(r, g, b):
    mu = jnp.mean(r, axis=-1, keepdims=True)
    d = r - mu
    var = jnp.mean(d * d, axis=-1, keepdims=True)
    return d * lax.rsqrt(var + LN_EPS) * g + b


def _ada_kernel(c_ref, w_ref, b_ref, o_ref):
    c = c_ref[...]
    o_ref[0, 0] = _dot(jax.nn.silu(c).astype(_BF), w_ref[0].astype(_BF)) + b_ref[0]


def _ada_params(c_all, w_ada, b_ada):
    n = c_all.shape[0]
    return pl.pallas_call(
        _ada_kernel,
        grid=(DEPTH, 3),
        in_specs=[pl.BlockSpec((n, D_MODEL), lambda l, j: (0, 0)),
                  pl.BlockSpec((1, D_MODEL, D_MODEL), lambda l, j: (l, 0, j)),
                  pl.BlockSpec((1, 1, D_MODEL), lambda l, j: (l, 0, j))],
        out_specs=pl.BlockSpec((1, 1, n, D_MODEL), lambda l, j: (l, j, 0, 0)),
        out_shape=jax.ShapeDtypeStruct((DEPTH, 3, n, D_MODEL), _F32),
        name="ada_params",
    )(c_all, w_ada, b_ada.reshape(DEPTH, 1, 3 * D_MODEL))


def _cmul(x_re, x_im, y_re, y_im):
    return x_re * y_re - x_im * y_im, x_re * y_im + x_im * y_re


def _zoh_a(a_re, a_im, log_dt):
    dt = jnp.exp(log_dt)
    mag = jnp.exp(a_re * dt)
    return mag * jnp.cos(a_im * dt), mag * jnp.sin(a_im * dt)


def _s5_prep_kernel(are_ref, aim_ref, ldt_ref, bre_ref, bim_ref, aret_ref, aimt_ref, ldtt_ref, cre_ref, cim_ref,
                    a2r_ref, a2i_ref, bbr_ref, bbi_ref, abr_ref, abi_ref, car_ref, cai_ref, ca2r_ref, ca2i_ref):
    a_re, a_im = are_ref[0], aim_ref[0]
    ab_re, ab_im = _zoh_a(a_re, a_im, ldt_ref[0])
    nr, ni = ab_re - 1.0, ab_im
    den = a_re * a_re + a_im * a_im
    f_re = (nr * a_re + ni * a_im) / den
    f_im = (ni * a_re - nr * a_im) / den
    bb_re, bb_im = _cmul(f_re, f_im, bre_ref[0], bim_ref[0])
    a2r_ref[0], a2i_ref[0] = _cmul(ab_re, ab_im, ab_re, ab_im)
    bbr_ref[0], bbi_ref[0] = bb_re, bb_im
    abr_ref[0], abi_ref[0] = _cmul(ab_re, ab_im, bb_re, bb_im)
    at_re, at_im = _zoh_a(aret_ref[0], aimt_ref[0], ldtt_ref[0])
    ca_re, ca_im = _cmul(cre_ref[0], cim_ref[0], at_re, at_im)
    car_ref[0], cai_ref[0] = ca_re, ca_im
    ca2r_ref[0], ca2i_ref[0] = _cmul(ca_re, ca_im, at_re, at_im)


def _s5_lag_kernel(cr_ref, ci_ref, car_ref, cai_ref, bbr_ref, bbi_ref, k0_ref, k1_ref):
    def re_prod(x_re, x_im):
        dot = lambda x, y: jnp.einsum('gcp,gpd->gcd', x, y, precision=lax.Precision.HIGHEST,
                                      preferred_element_type=_F32)
        return dot(x_re, bbr_ref[0]) - dot(x_im, bbi_ref[0])
    k0_ref[0] = re_prod(cr_ref[0], ci_ref[0])
    k1_ref[0] = re_prod(car_ref[0], cai_ref[0])


def _s5_prep(a_re, a_im, log_dt, b_re, b_im, c_re, c_im):
    nl, w = N_A_LAYERS, SSM_STATE * SSM_GROUP_CH
    flat = lambda a: a.reshape(nl, SSM_GROUPS, w)
    rep = lambda a: jnp.repeat(a, SSM_GROUP_CH, axis=-1)
    til = lambda a: jnp.tile(a, (1, 1, SSM_GROUP_CH))
    ldt = log_dt[:, :, None]
    spec = pl.BlockSpec((1, SSM_GROUPS, w), lambda l: (l, 0, 0))
    shp = jax.ShapeDtypeStruct((nl, SSM_GROUPS, w), _F32)
    a2_re, a2_im, bb_re, bb_im, abb_re, abb_im, ca_re, ca_im, ca2_re, ca2_im = pl.pallas_call(
        _s5_prep_kernel, grid=(nl,), in_specs=[spec] * 10, out_specs=[spec] * 10, out_shape=[shp] * 10, name="s5_prep",
    )(rep(a_re), rep(a_im), rep(jnp.broadcast_to(ldt, a_re.shape)), flat(b_re), flat(b_im),
      til(a_re), til(a_im), til(jnp.broadcast_to(ldt, a_re.shape)), flat(c_re), flat(c_im))
    pc = (nl, SSM_GROUPS, SSM_STATE, SSM_GROUP_CH)
    cp = (nl, SSM_GROUPS, SSM_GROUP_CH, SSM_STATE)
    cp_spec = pl.BlockSpec((1,) + cp[1:], lambda l: (l, 0, 0, 0))
    pc_spec = pl.BlockSpec((1,) + pc[1:], lambda l: (l, 0, 0, 0))
    kk = (nl, SSM_GROUPS, SSM_GROUP_CH, SSM_GROUP_CH)
    kk_spec = pl.BlockSpec((1,) + kk[1:], lambda l: (l, 0, 0, 0))
    k0, k1 = pl.pallas_call(
        _s5_lag_kernel, grid=(nl,), in_specs=[cp_spec] * 4 + [pc_spec] * 2, out_specs=[kk_spec] * 2,
        out_shape=[jax.ShapeDtypeStruct(kk, _F32)] * 2, name="s5_lag",
    )(c_re, c_im, ca_re.reshape(cp), ca_im.reshape(cp), bb_re.reshape(pc), bb_im.reshape(pc))
    return dict(a2_re=a2_re.reshape(pc)[..., 0], a2_im=a2_im.reshape(pc)[..., 0],
                bb_re=bb_re.reshape(pc), bb_im=bb_im.reshape(pc), abb_re=abb_re.reshape(pc), abb_im=abb_im.reshape(pc),
                ca_re=ca_re.reshape(cp), ca_im=ca_im.reshape(cp), ca2_re=ca2_re.reshape(cp), ca2_im=ca2_im.reshape(cp),
                k0=k0, k1=k1)


def _s5_matrices(q):
    same = jnp.eye(STRIP_GROUPS, dtype=bool)
    nl = N_A_LAYERS
    zero = jnp.zeros((), _BF)
    state_cols = np.arange(STRIP_W)
    spread = np.equal.outer(np.arange(2 * SSM_STATE),
                            (state_cols // STRIP_HALF) * SSM_STATE + state_cols % SSM_STATE)
    chan_rows = np.arange(2 * LANES)
    diag = np.equal.outer(chan_rows % LANES // SSM_GROUP_CH, state_cols % STRIP_HALF // SSM_STATE)
    spread_bf = jnp.asarray(spread, _BF)

    def compact(parts, order):
        x = jnp.stack([jnp.stack(ri, axis=0) for ri in parts], axis=0).astype(_BF)
        x = x.reshape(2, 2, nl, N_STRIPS, STRIP_GROUPS, x.shape[-2], x.shape[-1])
        return x.transpose(order)

    b_c = compact([(q['abb_re'], q['abb_im']), (q['bb_re'], q['bb_im'])], (2, 3, 0, 4, 6, 1, 5))
    b_c = b_c.reshape(nl, N_STRIPS, 2 * LANES, 2 * SSM_STATE)
    bs2 = jnp.where(diag, jnp.einsum('ljab,bn->ljan', b_c, spread_bf, preferred_element_type=_BF), zero)
    c_c = compact([(q['ca_re'], -q['ca_im']), (q['ca2_re'], -q['ca2_im'])], (2, 3, 1, 6, 0, 4, 5))
    c_c = c_c.reshape(nl, N_STRIPS, 2 * SSM_STATE, 2 * LANES)
    cs2 = jnp.where(diag.T, jnp.einsum('nb,ljba->ljna', spread_bf.T, c_c, preferred_element_type=_BF), zero)

    def lag(k):
        kt = k.astype(_BF).transpose(0, 1, 3, 2).reshape(nl, N_STRIPS, STRIP_GROUPS, SSM_GROUP_CH, SSM_GROUP_CH)
        placed = jnp.where(same[None, None, :, None, :, None], kt[:, :, :, :, None, :], zero)
        return placed.reshape(nl, N_STRIPS, LANES, LANES)

    k0, k1 = lag(q['k0']), lag(q['k1'])
    ks = jnp.concatenate([jnp.concatenate([k0, k1], axis=3),
                          jnp.concatenate([jnp.zeros_like(k0), k0], axis=3)], axis=2)
    a2r = q['a2_re'].reshape(nl, N_STRIPS, 1, STRIP_HALF)
    a2i = q['a2_im'].reshape(nl, N_STRIPS, 1, STRIP_HALF)
    return bs2, cs2, ks, a2r, a2i


def _state_to_strips(state):
    nb = state.shape[0]
    s = state.reshape(nb, N_STRIPS, STRIP_GROUPS, SSM_STATE, 2)
    return s.transpose(1, 0, 4, 2, 3).reshape(N_STRIPS, nb, STRIP_W)


def _strips_to_state(h):
    nb = h.shape[1]
    s = h.reshape(N_STRIPS, nb, 2, STRIP_GROUPS, SSM_STATE)
    return s.transpose(1, 0, 3, 4, 2).reshape(nb, SSM_GROUPS, SSM_STATE, 2)


def _s5_layer_kernel(x_ref, ada_ref, h0_ref, win_ref, bs_ref, cs_ref, ks_ref, ar_ref, ai_ref,
                     d_ref, wglu_ref, bglu_ref, wout_ref, lng_ref, lnb_ref,
                     y_ref, hout_ref,
                     h_scr, u_scr, z_scr, bu_scr, yy_scr, xt_scr, *, nb, steps, row0, batch_major_in, batch_major_out):
    rows = nb * steps
    shift, scale, gate = (ada_ref[k, row0:row0 + nb, :] for k in range(3))
    i = pl.program_id(0)

    @pl.when(i == 0)
    def _():
        h_scr[...] = h0_ref[...]

    lane_tiles = D_MODEL // LANES
    if batch_major_in:
        for b in range(nb):
            for c in range(lane_tiles):
                xt_scr[c, pl.ds(b, steps, stride=nb), :] = x_ref[b, :, c * LANES:(c + 1) * LANES]
        x_rows = jnp.concatenate([xt_scr[c] for c in range(lane_tiles)], axis=1)
    else:
        x_rows = x_ref[...]
    x3 = x_rows.reshape(steps, nb, D_MODEL)
    hmod = (x3 * (1.0 + scale)[None] + shift[None]).reshape(rows, D_MODEL)
    uz = _dot(hmod.astype(_BF), win_ref[...])
    pairs = steps // 2
    prow = pairs * nb
    for j in range(N_STRIPS):
        u_scr[j] = uz[:, j * LANES:(j + 1) * LANES].reshape(pairs, 2, nb, LANES)
    z_scr[...] = uz[:, D_MODEL:]

    def pair_inputs(j):
        return jnp.concatenate([u_scr[j, :, 0].reshape(prow, LANES), u_scr[j, :, 1].reshape(prow, LANES)], axis=1)

    def expand(j):
        bu_scr[j % 2] = _dot(pair_inputs(j).astype(_BF), bs_ref[j])

    def scan(j):
        buf = j % 2
        a_re_row, a_im_row = ar_ref[j], ai_ref[j]
        for s in range(STRIP_HALF // SCAN_W):
            re = slice(s * SCAN_W, (s + 1) * SCAN_W)
            im = slice(STRIP_HALF + s * SCAN_W, STRIP_HALF + (s + 1) * SCAN_W)
            a_re = jnp.broadcast_to(a_re_row[:, re], (nb, SCAN_W))
            a_im = jnp.broadcast_to(a_im_row[:, re], (nb, SCAN_W))
            h_re, h_im = h_scr[j, :, re], h_scr[j, :, im]
            for t in range(pairs):
                now = slice(t * nb, (t + 1) * nb)
                n_re = a_re * h_re - a_im * h_im + bu_scr[buf, now, re]
                n_im = a_re * h_im + a_im * h_re + bu_scr[buf, now, im]
                bu_scr[buf, now, re] = h_re
                bu_scr[buf, now, im] = h_im
                h_re, h_im = n_re, n_im
            h_scr[j, :, re] = h_re
            h_scr[j, :, im] = h_im

    def project(j):
        yy = _dot(bu_scr[j % 2].astype(_BF), cs_ref[j]) + _dot(pair_inputs(j).astype(_BF), ks_ref[j])
        for par in range(2):
            y_par = yy[:, par * LANES:(par + 1) * LANES].reshape(pairs, nb, LANES) + d_ref[j] * u_scr[j, :, par]
            yy_scr[j, :, par] = y_par

    expand(0)
    for j in range(N_STRIPS):
        if j + 1 < N_STRIPS:
            expand(j + 1)
        scan(j)
        project(j)

    halves = 2
    hs, hp, hr = steps // halves, pairs // halves, rows // halves
    ys = [jnp.concatenate([yy_scr[j, k * hp:(k + 1) * hp].reshape(hr, LANES) for j in range(N_STRIPS)], axis=1)
          for k in range(halves)]
    gs = [jax.nn.gelu(y) for y in ys]
    glu = [_dot(g.astype(_BF), wglu_ref[...]) for g in gs]
    outs = []
    for k in range(halves):
        y = gs[k] * jax.nn.sigmoid(glu[k] + bglu_ref[...]) * jax.nn.silu(z_scr[k * hr:(k + 1) * hr])
        outs.append(_dot(y.astype(_BF), wout_ref[...]).reshape(hs, nb, D_MODEL))
    for k in range(halves):
        r = (DEEPNORM_ALPHA * x3[k * hs:(k + 1) * hs] + gate[None] * outs[k]).reshape(hr, D_MODEL)
        y = _layer_norm(r, lng_ref[...], lnb_ref[...])
        if batch_major_out:
            for c in range(lane_tiles):
                xt_scr[c, k * hr:(k + 1) * hr] = y[:, c * LANES:(c + 1) * LANES]
            for b in range(nb):
                for c in range(lane_tiles):
                    y_ref[b, k * hs:(k + 1) * hs, c * LANES:(c + 1) * LANES] = (
                        xt_scr[c, pl.ds(k * hr + b, hs, stride=nb), :])
        else:
            y_ref[k * hr:(k + 1) * hr] = y

    @pl.when(i == pl.num_programs(0) - 1)
    def _():
        hout_ref[...] = h_scr[...]


def _s5_layer(x, nb, seq, ada_all, row0, h0, la, p, batch_major_in, batch_major_out):
    total = seq * nb
    rows = min(S5_ROWS, total)
    steps = rows // nb
    tb_spec = pl.BlockSpec((rows, D_MODEL), lambda i: (i, 0))
    bt_spec = pl.BlockSpec((nb, steps, D_MODEL), lambda i: (0, i, 0))
    tb_shape = jax.ShapeDtypeStruct((total, D_MODEL), _F32)
    bt_shape = jax.ShapeDtypeStruct((nb, seq, D_MODEL), _F32)
    kern = functools.partial(_s5_layer_kernel, nb=nb, steps=steps, row0=row0, batch_major_in=batch_major_in,
                             batch_major_out=batch_major_out)
    weights = [p[k] for k in ('win_a', 'bs', 'cs', 'ks', 'ar', 'ai', 'ssm_d', 'w_glu', 'b_glu', 'w_out_a', 'ln_g', 'ln_b')]
    return pl.pallas_call(
        kern,
        grid=(total // rows,),
        in_specs=[bt_spec if batch_major_in else tb_spec,
                  _layer_spec(ada_all.shape, la),
                  _layer_spec(h0.shape, la)] + [_layer_spec(w.shape, la) for w in weights],
        out_specs=[bt_spec if batch_major_out else tb_spec,
                   pl.BlockSpec((N_STRIPS, nb, STRIP_W), lambda i: (0, 0, 0))],
        out_shape=[bt_shape if batch_major_out else tb_shape,
                   jax.ShapeDtypeStruct((N_STRIPS, nb, STRIP_W), _F32)],
        scratch_shapes=[pltpu.VMEM((N_STRIPS, nb, STRIP_W), _F32),
                        pltpu.VMEM((N_STRIPS, steps // 2, 2, nb, LANES), _F32),
                        pltpu.VMEM((rows, D_MODEL), _F32),
                        pltpu.VMEM((2, rows // 2, STRIP_W), _F32),
                        pltpu.VMEM((N_STRIPS, steps // 2, 2, nb, LANES), _F32),
                        pltpu.VMEM((D_MODEL // LANES, rows, LANES), _F32)],
        compiler_params=pltpu.CompilerParams(dimension_semantics=("arbitrary",), vmem_limit_bytes=VMEM_LIMIT),
        name="s5_layer",
    )(x, ada_all, h0, *weights)


def _rope_angles(pos):
    half = HEAD_DIM // 2
    inv_freq = np.power(ROPE_THETA, -np.arange(half, dtype=np.float64) / half)
    ang = pos.astype(np.float64)[:, None] * inv_freq[None, :]
    return np.cos(ang).astype(np.float32), np.sin(ang).astype(np.float32)


def _rope_tables(pos):
    cos, sin = _rope_angles(pos)
    zero = np.zeros_like(sin)
    cos_t = np.tile(cos, (1, LANES // (HEAD_DIM // 2)))
    sin_lo = np.tile(np.concatenate([-sin, zero], axis=1), (1, LANES // HEAD_DIM))
    sin_hi = np.tile(np.concatenate([zero, sin], axis=1), (1, LANES // HEAD_DIM))
    return jnp.asarray(cos_t), jnp.asarray(sin_lo), jnp.asarray(sin_hi)


def _rope_tile(x, cos_t, sin_lo, sin_hi):
    half = HEAD_DIM // 2
    return x * cos_t + pltpu.roll(x, LANES - half, 1) * sin_lo + pltpu.roll(x, half, 1) * sin_hi


def _kv_kernel(x_ref, w_ref, cos_ref, slo_ref, shi_ref, k_ref, v_ref, kbf_ref):
    kv = _dot(x_ref[0].astype(_BF), w_ref[...])
    cos_t, sin_lo, sin_hi = cos_ref[...], slo_ref[...], shi_ref[...]
    for t in range(KV_WIDTH // LANES):
        tile = _rope_tile(kv[:, t * LANES:(t + 1) * LANES], cos_t, sin_lo, sin_hi)
        k_ref[0, :, t * LANES:(t + 1) * LANES] = tile
        kbf_ref[0, :, t * LANES:(t + 1) * LANES] = tile.astype(_BF)
    v_ref[0] = kv[:, KV_WIDTH:]


def _shared_kv(x, w_kv, tables):
    bsz, seq, _ = x.shape
    rows = min(KV_ROWS, seq)
    tab_spec = pl.BlockSpec((rows, LANES), lambda b, i: (i, 0))
    kv_spec = pl.BlockSpec((1, rows, KV_WIDTH), lambda b, i: (b, i, 0))
    out_specs = [kv_spec] * 3
    out_shape = ([jax.ShapeDtypeStruct((bsz, seq, KV_WIDTH), _F32)] * 2
                 + [jax.ShapeDtypeStruct((bsz, seq, KV_WIDTH), _BF)])
    return pl.pallas_call(
        _kv_kernel,
        grid=(bsz, seq // rows),
        in_specs=[pl.BlockSpec((1, rows, D_MODEL), lambda b, i: (b, i, 0)),
                  _const_spec((D_MODEL, 2 * KV_WIDTH)), tab_spec, tab_spec, tab_spec],
        out_specs=out_specs, out_shape=out_shape,
        compiler_params=pltpu.CompilerParams(dimension_semantics=("arbitrary", "arbitrary"),
                                             vmem_limit_bytes=VMEM_LIMIT),
        name="shared_kv",
    )(x, w_kv, *tables)


def _reduce_rows(x, pair_op, final_op):
    sub = 8
    parts = [x[i * sub:(i + 1) * sub] for i in range(x.shape[0] // sub)]
    while len(parts) > 1:
        parts = [pair_op(parts[i], parts[i + 1]) for i in range(0, len(parts), 2)]
    return final_op(parts[0], axis=0, keepdims=True)


def _attn_layer_kernel(sinks_ref, x_ref, ada_ref, win_ref, wout_ref, cos_ref, sin_ref,
                       lng_ref, lnb_ref, *refs, bb, rb, row0, layer_b, banded, make_kv):
    rows = bb * rb
    n_pairs = rows // LANES
    pairs_per_stream = rb // LANES
    if make_kv:
        (wkv_ref, cosr_ref, slo_ref, shi_ref, y_ref, kcache_ref, vcache_ref, kbf_ref, vtout_ref,
         hm_scr, blk_scr, qt_scr, gt_scr, ot_scr, ob_scr, acc_scr, cap_scr, k_src, vt_src) = refs
        step = pl.program_id(1)
        kv_new = _dot(x_ref[0].astype(_BF), wkv_ref[...])
        first_row = pl.multiple_of(step * rb, rb)
        for t in range(KV_WIDTH // LANES):
            tile = _rope_tile(kv_new[:, t * LANES:(t + 1) * LANES], cosr_ref[...], slo_ref[...], shi_ref[...])
            kcache_ref[0, :, t * LANES:(t + 1) * LANES] = tile[rb - WINDOW:]
            kbf_ref[0, :, t * LANES:(t + 1) * LANES] = tile.astype(_BF)
            k_src[0, pl.ds(first_row, rb), t * LANES:(t + 1) * LANES] = tile.astype(_BF)
        v_new = kv_new[:, KV_WIDTH:]
        vcache_ref[0] = v_new[rb - WINDOW:]
        for blk in range(rb // LANES):
            vt_blk = v_new[blk * LANES:(blk + 1) * LANES].T.astype(_BF)
            vtout_ref[0, blk] = vt_blk
            vt_src[0, step * (rb // LANES) + blk] = vt_blk
    else:
        (k_src, vt_src, y_ref, hm_scr, blk_scr, qt_scr, gt_scr, ot_scr, ob_scr, acc_scr, cap_scr) = refs

    def ada_row(k, b):
        if banded:
            return ada_ref[k, pl.ds(row0 + pl.program_id(0) * bb + b, 1), :]
        return ada_ref[k, row0 + b:row0 + b + 1, :]

    for b in range(bb):
        hm_scr[b * rb:(b + 1) * rb] = (x_ref[b] * (1.0 + ada_row(1, b)) + ada_row(0, b)).astype(_BF)
    cos_t, sin_t = cos_ref[...], sin_ref[...]
    half = HEAD_DIM // 2
    qk_scale = HEAD_DIM ** -0.5 * LOG2_E
    kt = IN_PIECE

    def in_piece(c, k):
        part = _dot(hm_scr[:, k * kt:(k + 1) * kt], win_ref[k * kt:(k + 1) * kt, c * HEAD_TILE:(c + 1) * HEAD_TILE])
        if k == 0:
            blk_scr[c // N_KV_HEADS] = part
        else:
            blk_scr[c // N_KV_HEADS] += part

    def in_finish(c):
        t = blk_scr[c // N_KV_HEADS].T
        if c < N_KV_HEADS:
            for j in range(Q_PER_KV):
                lo = c * HEAD_TILE + j * HEAD_DIM
                x1, x2 = t[j * HEAD_DIM:j * HEAD_DIM + half], t[j * HEAD_DIM + half:(j + 1) * HEAD_DIM]
                qt_scr[lo:lo + half] = ((x1 * cos_t - x2 * sin_t) * qk_scale).astype(_BF)
                qt_scr[lo + half:lo + HEAD_DIM] = ((x2 * cos_t + x1 * sin_t) * qk_scale).astype(_BF)
        else:
            lo = (c - N_KV_HEADS) * HEAD_TILE
            gt_scr[lo:lo + HEAD_TILE] = jax.nn.silu(t)

    def out_prepare(kv):
        ob_scr[...] = ot_scr[kv * HEAD_TILE:(kv + 1) * HEAD_TILE].T.astype(_BF)

    def out_piece(kv, n):
        part = _dot(ob_scr[...], wout_ref[kv * HEAD_TILE:(kv + 1) * HEAD_TILE, n * HEAD_TILE:(n + 1) * HEAD_TILE])
        if kv == 0:
            acc_scr[:, n * HEAD_TILE:(n + 1) * HEAD_TILE] = part
        else:
            acc_scr[:, n * HEAD_TILE:(n + 1) * HEAD_TILE] += part

    key_blk = lax.broadcasted_iota(jnp.int32, (PAIR_KEYS, LANES), 0) // CHUNK
    qry_blk = lax.broadcasted_iota(jnp.int32, (PAIR_KEYS, LANES), 1) // CHUNK
    no_q = jnp.zeros((HEAD_DIM, LANES), _BF)
    pairs = []
    for a in range(n_pairs):
        if banded:
            qc0 = pl.program_id(1) * (rb // CHUNK) + 2 * (a % pairs_per_stream)
            kc0 = jnp.maximum(qc0 - WINDOW_CHUNKS, 0)
            delta0 = kc0 - qc0
            kstart = pl.multiple_of(kc0 * CHUNK, LANES)
            vblk = kc0 // 2
        else:
            delta0, kstart, vblk = -WINDOW_CHUNKS, 0, 0
        rel = delta0 + key_blk - qry_blk
        cap_scr[a] = jnp.where((rel >= -WINDOW_CHUNKS) & (rel <= 0), MASK_PASS, NEG_INF)
        pairs.append((a // pairs_per_stream, kstart, vblk))

    def unit_heads(kv, gp):
        return (kv * Q_PER_KV + 2 * gp, kv * Q_PER_KV + 2 * gp + 1)

    def scores(a, kv, gp):
        lanes = slice(a * LANES, (a + 1) * LANES)
        stream, kstart, _ = pairs[a]
        k_tile = k_src[stream, pl.ds(kstart, PAIR_KEYS), (kv // 2) * LANES:(kv // 2 + 1) * LANES]
        w_parts = []
        for h in unit_heads(kv, gp):
            q_h = qt_scr[h * HEAD_DIM:(h + 1) * HEAD_DIM, lanes]
            w_parts.append(jnp.concatenate([q_h, no_q] if kv % 2 == 0 else [no_q, q_h], axis=0))
        return _dot(k_tile, jnp.concatenate(w_parts, axis=1))

    def softmax_values(a, kv, gp, s):
        e_parts, inv_den = [], []
        for u, h in enumerate(unit_heads(kv, gp)):
            s_h = jnp.minimum(s[:, u * LANES:(u + 1) * LANES], cap_scr[a])
            sink = sinks_ref[layer_b, h] * LOG2_E
            m = jnp.maximum(_reduce_rows(s_h, jnp.maximum, jnp.max), sink)
            e = jnp.exp2(s_h - m)
            inv_den.append(1.0 / (_reduce_rows(e, jnp.add, jnp.sum) + jnp.exp2(sink - m)))
            e_parts.append(e.astype(_BF))
        stream, _, vblk = pairs[a]
        vt_kv = jnp.concatenate([vt_src[stream, vblk, kv * HEAD_DIM:(kv + 1) * HEAD_DIM, :],
                                 vt_src[stream, vblk + 1, kv * HEAD_DIM:(kv + 1) * HEAD_DIM, :]], axis=1)
        return _dot(vt_kv, jnp.concatenate(e_parts, axis=1)), inv_den

    def finish(a, kv, gp, ot, inv_den):
        lanes = slice(a * LANES, (a + 1) * LANES)
        for u, h in enumerate(unit_heads(kv, gp)):
            rows_h = slice(h * HEAD_DIM, (h + 1) * HEAD_DIM)
            ot_scr[rows_h, lanes] = ot[:, u * LANES:(u + 1) * LANES] * inv_den[u] * gt_scr[rows_h, lanes]

    n_k = D_MODEL // kt

    def in_block(c):
        return [functools.partial(in_piece, c, k) for k in range(n_k)] + [functools.partial(in_finish, c)]

    for item in in_block(0) + in_block(N_KV_HEADS):
        item()
    slots = []
    for kv in range(N_KV_HEADS):
        units = [(a, kv, gp) for a in range(n_pairs) for gp in range(Q_PER_KV // 2)]
        work = [[] for _ in units]
        ahead, behind = [], []
        if kv + 1 < N_KV_HEADS:
            ahead = in_block(kv + 1) + in_block(kv + 1 + N_KV_HEADS)
        if kv > 0:
            behind = [functools.partial(out_prepare, kv - 1)] + [
                functools.partial(out_piece, kv - 1, n) for n in range(D_MODEL // HEAD_TILE)]
        for m, item in enumerate(ahead):
            work[m * len(units) // len(ahead)].append(item)
        for m, item in enumerate(behind):
            work[1 + m * (len(units) - 1) // len(behind)].append(item)
        slots += list(zip(units, work))
    s_next, unfinished = scores(*slots[0][0]), None
    for i, (unit, work) in enumerate(slots):
        for item in work:
            item()
        s_cur = s_next
        if i + 1 < len(slots):
            s_next = scores(*slots[i + 1][0])
        ot, inv_den = softmax_values(*unit, s_cur)
        if unfinished is not None:
            finish(*unfinished)
        unfinished = (*unit, ot, inv_den)
    finish(*unfinished)
    out_prepare(N_KV_HEADS - 1)
    half_rows = rows // 2
    seg = min(rb, half_rows)
    w_last = wout_ref[(N_KV_HEADS - 1) * HEAD_TILE:N_KV_HEADS * HEAD_TILE, :]
    outs = [acc_scr[k * half_rows:(k + 1) * half_rows] + _dot(ob_scr[k * half_rows:(k + 1) * half_rows], w_last)
            for k in range(2)]
    for k in range(2):
        for s in range(half_rows // seg):
            r0 = k * half_rows + s * seg
            b, q0 = r0 // rb, r0 % rb
            r = DEEPNORM_ALPHA * x_ref[b, q0:q0 + seg] + ada_row(2, b) * outs[k][s * seg:(s + 1) * seg]
            y_ref[b, q0:q0 + seg] = _layer_norm(r, lng_ref[...], lnb_ref[...])


def _attn_layer(x, ada_all, row0, lb, p, tables_t, banded, bb, rb, kv=None, kv_maker=None):
    bsz, seq, _ = x.shape
    rows = bb * rb
    layer = N_A_LAYERS + lb
    row_spec = pl.BlockSpec((bb, rb, D_MODEL), lambda b, i: (b, i, 0))
    tab_spec = pl.BlockSpec((HEAD_DIM // 2, rows), lambda b, i: (0, i))
    y_shape = jax.ShapeDtypeStruct((bsz, seq, D_MODEL), _F32)
    kern = functools.partial(_attn_layer_kernel, bb=bb, rb=rb, row0=row0, layer_b=lb, banded=banded,
                             make_kv=kv is None)
    if kv is None:
        assert bb == 1
        w_kv, row_tables = kv_maker
        rtab_spec = pl.BlockSpec((rb, LANES), lambda b, i: (i, 0))
        kv_in, kv_specs = (w_kv, *row_tables), [_const_spec((D_MODEL, 2 * KV_WIDTH))] + [rtab_spec] * 3
        cache_spec = pl.BlockSpec((1, WINDOW, KV_WIDTH), lambda b, i: (b, 0, 0))
        out_specs = [row_spec, cache_spec, cache_spec,
                     pl.BlockSpec((1, rb, KV_WIDTH), lambda b, i: (b, i, 0)),
                     pl.BlockSpec((1, rb // LANES, KV_WIDTH, LANES), lambda b, i: (b, i, 0, 0))]
        out_shape = [y_shape] + [jax.ShapeDtypeStruct((bsz, WINDOW, KV_WIDTH), _F32)] * 2 + [
            jax.ShapeDtypeStruct((bsz, seq, KV_WIDTH), _BF),
            jax.ShapeDtypeStruct((bsz, seq // LANES, KV_WIDTH, LANES), _BF)]
        kv_scratch = [pltpu.VMEM((1, seq, KV_WIDTH), _BF), pltpu.VMEM((1, seq // LANES, KV_WIDTH, LANES), _BF)]
    else:
        keys = kv[0].shape[1]
        kv_in, kv_specs = kv, [pl.BlockSpec((bb, keys, KV_WIDTH), lambda b, i: (b, 0, 0)),
                               pl.BlockSpec((bb, keys // LANES, KV_WIDTH, LANES), lambda b, i: (b, 0, 0, 0))]
        out_specs, out_shape, kv_scratch = row_spec, y_shape, []
    return pl.pallas_call(
        kern,
        grid=(bsz // bb, seq // rb),
        in_specs=[pl.BlockSpec(memory_space=pltpu.SMEM),
                  row_spec, _layer_spec(ada_all.shape, layer),
                  _layer_spec(p['win_b'].shape, lb), _layer_spec(p['w_out_b'].shape, lb),
                  tab_spec, tab_spec,
                  _layer_spec(p['ln_g'].shape, layer), _layer_spec(p['ln_b'].shape, layer)] + kv_specs,
        out_specs=out_specs,
        out_shape=out_shape,
        scratch_shapes=[pltpu.VMEM((rows, D_MODEL), _BF),
                        pltpu.VMEM((2, rows, HEAD_TILE), _F32),
                        pltpu.VMEM((D_MODEL, rows), _BF),
                        pltpu.VMEM((D_MODEL, rows), _F32),
                        pltpu.VMEM((D_MODEL, rows), _F32),
                        pltpu.VMEM((rows, HEAD_TILE), _BF),
                        pltpu.VMEM((rows, D_MODEL), _F32),
                        pltpu.VMEM((rows // LANES, PAIR_KEYS, LANES), _F32)]
        + kv_scratch,
        compiler_params=pltpu.CompilerParams(dimension_semantics=("arbitrary", "arbitrary"),
                                             vmem_limit_bytes=VMEM_LIMIT),
        name="attn_layer",
    )(p['attn_sinks'], x, ada_all, p['win_b'], p['w_out_b'], *tables_t, p['ln_g'], p['ln_b'], *kv_in)


def _run_trunk(x, ada_all, row0, pos, h0, cache_k, cache_v, p):
    bsz, seq, _ = x.shape
    states = []
    for la in range(N_A_LAYERS):
        x, h_last = _s5_layer(x, bsz, seq, ada_all, row0, h0, la, p,
                              batch_major_in=la == 0, batch_major_out=la == N_A_LAYERS - 1)
        states.append(_strips_to_state(h_last))
    banded = cache_k is None
    row_tables = _rope_tables(pos)
    if banded:
        kv, kv_maker, q_pos = None, (p['w_kv'], row_tables), pos
        bb, rb = 1, ATTN_ROWS
    else:
        k_new, v_new, k_bf = _shared_kv(x, p['w_kv'], row_tables)
        pad = LANES - seq
        flat = lambda c: c.reshape(bsz, c.shape[1], KV_WIDTH)
        k_bf = jnp.concatenate([flat(cache_k).astype(_BF), k_bf, jnp.zeros((bsz, pad, KV_WIDTH), _BF)], axis=1)
        v_all = jnp.concatenate([flat(cache_v), v_new, jnp.zeros((bsz, pad, KV_WIDTH), _F32)], axis=1)
        vt = v_all.reshape(bsz, PAIR_KEYS // LANES, LANES, KV_WIDTH).transpose(0, 1, 3, 2).astype(_BF)
        kv, kv_maker = (k_bf, vt), None
        x = jnp.pad(x, ((0, 0), (0, pad), (0, 0)))
        q_pos = np.tile(pos[0] + np.arange(LANES), bsz)
        bb, rb = bsz, LANES
    cos, sin = _rope_angles(q_pos)
    tables_t = (jnp.asarray(cos.T), jnp.asarray(sin.T))
    for lb in range(N_B_LAYERS):
        out = _attn_layer(x, ada_all, row0, lb, p, tables_t, banded=banded, bb=bb, rb=rb, kv=kv, kv_maker=kv_maker)
        if kv is None:
            x, k_new, v_new, k_bf, vt = out
            kv = (k_bf, vt)
        else:
            x = out
    k4 = k_new.reshape(bsz, k_new.shape[1], N_KV_HEADS, HEAD_DIM)
    v4 = v_new.reshape(bsz, v_new.shape[1], N_KV_HEADS, HEAD_DIM)
    return x[:, :seq], jnp.stack(states), k4, v4


def kernel(x_prompt, x_sample, state_ssm, cache_k, cache_v, c_prompt, c_sample, w_ada, b_ada, ln_g, ln_b, w_in_a,
           ssm_a_re, ssm_a_im, ssm_b_re, ssm_b_im, ssm_c_re, ssm_c_im, ssm_d, ssm_log_dt, w_glu, b_glu, w_out_a,
           w_kv, w_in_b, attn_sinks, w_out_b):
    n_prompt, n_sample = x_prompt.shape[0], x_sample.shape[0]
    ada_all = _ada_params(jnp.concatenate([c_prompt, c_sample], axis=0), w_ada, b_ada)
    bs, cs, ks, ar, ai = _s5_matrices(
        _s5_prep(ssm_a_re, ssm_a_im, ssm_log_dt, ssm_b_re, ssm_b_im, ssm_c_re, ssm_c_im))
    p = dict(win_a=w_in_a.astype(_BF), bs=bs, cs=cs, ks=ks, ar=ar, ai=ai,
             ssm_d=ssm_d.reshape(N_A_LAYERS, N_STRIPS, 1, LANES), w_glu=w_glu.astype(_BF),
             b_glu=b_glu.reshape(N_A_LAYERS, 1, D_MODEL), w_out_a=w_out_a.astype(_BF),
             ln_g=ln_g.reshape(DEPTH, 1, D_MODEL), ln_b=ln_b.reshape(DEPTH, 1, D_MODEL), w_kv=w_kv.astype(_BF),
             win_b=w_in_b.astype(_BF), w_out_b=w_out_b.astype(_BF), attn_sinks=attn_sinks)

    pos_prompt = np.arange(x_prompt.shape[1])
    pos_sample = PAST_LEN + np.arange(x_sample.shape[1])
    h0_prompt = jnp.zeros((N_A_LAYERS, N_STRIPS, n_prompt, STRIP_W), _F32)
    h0_sample = jnp.stack([_state_to_strips(state_ssm[la]) for la in range(N_A_LAYERS)])

    y_p, ssm_p, k_p, v_p = _run_trunk(x_prompt, ada_all, 0, pos_prompt, h0_prompt, None, None, p)
    y_s, ssm_s, k_s, v_s = _run_trunk(x_sample, ada_all, n_prompt, pos_sample, h0_sample, cache_k, cache_v, p)
    rows = min(WINDOW, x_prompt.shape[1])
    return (y_p, y_s, ssm_p, k_p[:, -rows:], v_p[:, -rows:], ssm_s, k_s, v_s)
```

```python
import functools
import math

import jax
import jax.numpy as jnp
import numpy as np
from jax import lax
from jax.experimental import pallas as pl
from jax.experimental.pallas import tpu as pltpu

D_MODEL = 1024
DEPTH = 4
CHUNK = 64
N_A_LAYERS = 2
N_B_LAYERS = 2
SSM_GROUP_CH = 16
SSM_GROUPS = 64
SSM_STATE = 64
HEAD_DIM = 64
N_HEADS = 16
N_KV_HEADS = 4
Q_PER_KV = 4
KV_WIDTH = N_KV_HEADS * HEAD_DIM
WINDOW = 128
WINDOW_CHUNKS = WINDOW // CHUNK
PAST_LEN = 1024
ROPE_THETA = 10000.0
NEG_INF = -1e30
DEEPNORM_ALPHA = (2.0 * DEPTH) ** 0.25
LN_EPS = 1e-5

LANES = 128
STRIP_GROUPS = LANES // SSM_GROUP_CH
N_STRIPS = SSM_GROUPS // STRIP_GROUPS
STRIP_HALF = STRIP_GROUPS * SSM_STATE
STRIP_W = 2 * STRIP_HALF
SCAN_W = 256
S5_ROWS = 512
KEYS = (WINDOW_CHUNKS + 1) * CHUNK
PAIR_KEYS = KEYS + CHUNK
ATTN_ROWS = 512
HEAD_TILE = Q_PER_KV * HEAD_DIM
IN_PIECE = 512
SCORE_AHEAD = 3
KV_ROWS = 512
MASK_PASS = 3.0e38
LOG2_E = math.log2(math.e)
VMEM_LIMIT = 56 * 1024 * 1024

_BF = jnp.bfloat16
_F32 = jnp.float32


def _dot(a, b):
    return jnp.dot(a, b, preferred_element_type=_F32)


def _const_spec(shape):
    nd = len(shape)
    return pl.BlockSpec(shape, lambda *_: (0,) * nd, pipeline_mode=pl.Buffered(1))


def _layer_spec(shape, layer):
    nd = len(shape)
    return pl.BlockSpec((None,) + tuple(shape[1:]), lambda *_: (layer,) + (0,) * (nd - 1),
                        pipeline_mode=pl.Buffered(1))


def _layer_norm(r, g, b):
    mu = jnp.mean(r, axis=-1, keepdims=True)
    d = r - mu
    var = jnp.mean(d * d, axis=-1, keepdims=True)
    return d * lax.rsqrt(var + LN_EPS) * g + b


def _ada_kernel(c_ref, w_ref, b_ref, o_ref):
    c = c_ref[...]
    o_ref[0, 0] = _dot(jax.nn.silu(c).astype(_BF), w_ref[0].astype(_BF)) + b_ref[0]


def _ada_params(c_all, w_ada, b_ada):
    n = c_all.shape[0]
    return pl.pallas_call(
        _ada_kernel,
        grid=(DEPTH, 3),
        in_specs=[pl.BlockSpec((n, D_MODEL), lambda l, j: (0, 0)),
                  pl.BlockSpec((1, D_MODEL, D_MODEL), lambda l, j: (l, 0, j)),
                  pl.BlockSpec((1, 1, D_MODEL), lambda l, j: (l, 0, j))],
        out_specs=pl.BlockSpec((1, 1, n, D_MODEL), lambda l, j: (l, j, 0, 0)),
        out_shape=jax.ShapeDtypeStruct((DEPTH, 3, n, D_MODEL), _F32),
        name="ada_params",
    )(c_all, w_ada, b_ada.reshape(DEPTH, 1, 3 * D_MODEL))


def _cmul(x_re, x_im, y_re, y_im):
    return x_re * y_re - x_im * y_im, x_re * y_im + x_im * y_re


def _zoh_a(a_re, a_im, log_dt):
    dt = jnp.exp(log_dt)
    mag = jnp.exp(a_re * dt)
    return mag * jnp.cos(a_im * dt), mag * jnp.sin(a_im * dt)


def _s5_prep_kernel(are_ref, aim_ref, ldt_ref, bre_ref, bim_ref, aret_ref, aimt_ref, ldtt_ref, cre_ref, cim_ref,
                    a2r_ref, a2i_ref, bbr_ref, bbi_ref, abr_ref, abi_ref, car_ref, cai_ref, ca2r_ref, ca2i_ref):
    a_re, a_im = are_ref[0], aim_ref[0]
    ab_re, ab_im = _zoh_a(a_re, a_im, ldt_ref[0])
    nr, ni = ab_re - 1.0, ab_im
    den = a_re * a_re + a_im * a_im
    f_re = (nr * a_re + ni * a_im) / den
    f_im = (ni * a_re - nr * a_im) / den
    bb_re, bb_im = _cmul(f_re, f_im, bre_ref[0], bim_ref[0])
    a2r_ref[0], a2i_ref[0] = _cmul(ab_re, ab_im, ab_re, ab_im)
    bbr_ref[0], bbi_ref[0] = bb_re, bb_im
    abr_ref[0], abi_ref[0] = _cmul(ab_re, ab_im, bb_re, bb_im)
    at_re, at_im = _zoh_a(aret_ref[0], aimt_ref[0], ldtt_ref[0])
    ca_re, ca_im = _cmul(cre_ref[0], cim_ref[0], at_re, at_im)
    car_ref[0], cai_ref[0] = ca_re, ca_im
    ca2r_ref[0], ca2i_ref[0] = _cmul(ca_re, ca_im, at_re, at_im)


def _s5_lag_kernel(cr_ref, ci_ref, car_ref, cai_ref, bbr_ref, bbi_ref, k0_ref, k1_ref):
    def re_prod(x_re, x_im):
        dot = lambda x, y: jnp.einsum('gcp,gpd->gcd', x, y, precision=lax.Precision.HIGHEST,
                                      preferred_element_type=_F32)
        return dot(x_re, bbr_ref[0]) - dot(x_im, bbi_ref[0])
    k0_ref[0] = re_prod(cr_ref[0], ci_ref[0])
    k1_ref[0] = re_prod(car_ref[0], cai_ref[0])


def _s5_prep(a_re, a_im, log_dt, b_re, b_im, c_re, c_im):
    nl, w = N_A_LAYERS, SSM_STATE * SSM_GROUP_CH
    flat = lambda a: a.reshape(nl, SSM_GROUPS, w)
    rep = lambda a: jnp.repeat(a, SSM_GROUP_CH, axis=-1)
    til = lambda a: jnp.tile(a, (1, 1, SSM_GROUP_CH))
    ldt = log_dt[:, :, None]
    spec = pl.BlockSpec((1, SSM_GROUPS, w), lambda l: (l, 0, 0))
    shp = jax.ShapeDtypeStruct((nl, SSM_GROUPS, w), _F32)
    a2_re, a2_im, bb_re, bb_im, abb_re, abb_im, ca_re, ca_im, ca2_re, ca2_im = pl.pallas_call(
        _s5_prep_kernel, grid=(nl,), in_specs=[spec] * 10, out_specs=[spec] * 10, out_shape=[shp] * 10, name="s5_prep",
    )(rep(a_re), rep(a_im), rep(jnp.broadcast_to(ldt, a_re.shape)), flat(b_re), flat(b_im),
      til(a_re), til(a_im), til(jnp.broadcast_to(ldt, a_re.shape)), flat(c_re), flat(c_im))
    pc = (nl, SSM_GROUPS, SSM_STATE, SSM_GROUP_CH)
    cp = (nl, SSM_GROUPS, SSM_GROUP_CH, SSM_STATE)
    cp_spec = pl.BlockSpec((1,) + cp[1:], lambda l: (l, 0, 0, 0))
    pc_spec = pl.BlockSpec((1,) + pc[1:], lambda l: (l, 0, 0, 0))
    kk = (nl, SSM_GROUPS, SSM_GROUP_CH, SSM_GROUP_CH)
    kk_spec = pl.BlockSpec((1,) + kk[1:], lambda l: (l, 0, 0, 0))
    k0, k1 = pl.pallas_call(
        _s5_lag_kernel, grid=(nl,), in_specs=[cp_spec] * 4 + [pc_spec] * 2, out_specs=[kk_spec] * 2,
        out_shape=[jax.ShapeDtypeStruct(kk, _F32)] * 2, name="s5_lag",
    )(c_re, c_im, ca_re.reshape(cp), ca_im.reshape(cp), bb_re.reshape(pc), bb_im.reshape(pc))
    return dict(a2_re=a2_re.reshape(pc)[..., 0], a2_im=a2_im.reshape(pc)[..., 0],
                bb_re=bb_re.reshape(pc), bb_im=bb_im.reshape(pc), abb_re=abb_re.reshape(pc), abb_im=abb_im.reshape(pc),
                ca_re=ca_re.reshape(cp), ca_im=ca_im.reshape(cp), ca2_re=ca2_re.reshape(cp), ca2_im=ca2_im.reshape(cp),
                k0=k0, k1=k1)


def _s5_matrices(q):
    same = jnp.eye(STRIP_GROUPS, dtype=bool)
    nl = N_A_LAYERS
    zero = jnp.zeros((), _BF)
    state_cols = np.arange(STRIP_W)
    spread = np.equal.outer(np.arange(2 * SSM_STATE),
                            (state_cols // STRIP_HALF) * SSM_STATE + state_cols % SSM_STATE)
    chan_rows = np.arange(2 * LANES)
    diag = np.equal.outer(chan_rows % LANES // SSM_GROUP_CH, state_cols % STRIP_HALF // SSM_STATE)
    spread_bf = jnp.asarray(spread, _BF)

    def compact(parts, order):
        x = jnp.stack([jnp.stack(ri, axis=0) for ri in parts], axis=0).astype(_BF)
        x = x.reshape(2, 2, nl, N_STRIPS, STRIP_GROUPS, x.shape[-2], x.shape[-1])
        return x.transpose(order)

    b_c = compact([(q['abb_re'], q['abb_im']), (q['bb_re'], q['bb_im'])], (2, 3, 0, 4, 6, 1, 5))
    b_c = b_c.reshape(nl, N_STRIPS, 2 * LANES, 2 * SSM_STATE)
    bs2 = jnp.where(diag, jnp.einsum('ljab,bn->ljan', b_c, spread_bf, preferred_element_type=_BF), zero)
    c_c = compact([(q['ca_re'], -q['ca_im']), (q['ca2_re'], -q['ca2_im'])], (2, 3, 1, 6, 0, 4, 5))
    c_c = c_c.reshape(nl, N_STRIPS, 2 * SSM_STATE, 2 * LANES)
    cs2 = jnp.where(diag.T, jnp.einsum('nb,ljba->ljna', spread_bf.T, c_c, preferred_element_type=_BF), zero)

    def lag(k):
        kt = k.astype(_BF).transpose(0, 1, 3, 2).reshape(nl, N_STRIPS, STRIP_GROUPS, SSM_GROUP_CH, SSM_GROUP_CH)
        placed = jnp.where(same[None, None, :, None, :, None], kt[:, :, :, :, None, :], zero)
        return placed.reshape(nl, N_STRIPS, LANES, LANES)

    k0, k1 = lag(q['k0']), lag(q['k1'])
    ks = jnp.concatenate([jnp.concatenate([k0, k1], axis=3),
                          jnp.concatenate([jnp.zeros_like(k0), k0], axis=3)], axis=2)
    a2r = q['a2_re'].reshape(nl, N_STRIPS, 1, STRIP_HALF)
    a2i = q['a2_im'].reshape(nl, N_STRIPS, 1, STRIP_HALF)
    return bs2, cs2, ks, a2r, a2i


def _state_to_strips(state):
    nb = state.shape[0]
    s = state.reshape(nb, N_STRIPS, STRIP_GROUPS, SSM_STATE, 2)
    return s.transpose(1, 0, 4, 2, 3).reshape(N_STRIPS, nb, STRIP_W)


def _strips_to_state(h):
    nb = h.shape[1]
    s = h.reshape(N_STRIPS, nb, 2, STRIP_GROUPS, SSM_STATE)
    return s.transpose(1, 0, 3, 4, 2).reshape(nb, SSM_GROUPS, SSM_STATE, 2)


def _s5_layer_kernel(x_ref, ada_ref, h0_ref, win_ref, bs_ref, cs_ref, ks_ref, ar_ref, ai_ref,
                     d_ref, wglu_ref, bglu_ref, wout_ref, lng_ref, lnb_ref,
                     y_ref, hout_ref,
                     h_scr, u_scr, z_scr, bu_scr, yy_scr, xt_scr, *, nb, steps, row0, batch_major_in, batch_major_out):
    rows = nb * steps
    shift, scale, gate = (ada_ref[k, row0:row0 + nb, :] for k in range(3))
    i = pl.program_id(0)

    @pl.when(i == 0)
    def _():
        h_scr[...] = h0_ref[...]

    lane_tiles = D_MODEL // LANES
    if batch_major_in:
        for b in range(nb):
            for c in range(lane_tiles):
                xt_scr[c, pl.ds(b, steps, stride=nb), :] = x_ref[b, :, c * LANES:(c + 1) * LANES]
        x_rows = jnp.concatenate([xt_scr[c] for c in range(lane_tiles)], axis=1)
    else:
        x_rows = x_ref[...]
    x3 = x_rows.reshape(steps, nb, D_MODEL)
    hmod = (x3 * (1.0 + scale)[None] + shift[None]).reshape(rows, D_MODEL)
    uz = _dot(hmod.astype(_BF), win_ref[...])
    pairs = steps // 2
    prow = pairs * nb
    for j in range(N_STRIPS):
        u_scr[j] = uz[:, j * LANES:(j + 1) * LANES].reshape(pairs, 2, nb, LANES)
    z_scr[...] = uz[:, D_MODEL:]

    def pair_inputs(j):
        return jnp.concatenate([u_scr[j, :, 0].reshape(prow, LANES), u_scr[j, :, 1].reshape(prow, LANES)], axis=1)

    def expand(j):
        bu_scr[j % 2] = _dot(pair_inputs(j).astype(_BF), bs_ref[j])

    def scan(j):
        buf = j % 2
        a_re_row, a_im_row = ar_ref[j], ai_ref[j]
        for s in range(STRIP_HALF // SCAN_W):
            re = slice(s * SCAN_W, (s + 1) * SCAN_W)
            im = slice(STRIP_HALF + s * SCAN_W, STRIP_HALF + (s + 1) * SCAN_W)
            a_re = jnp.broadcast_to(a_re_row[:, re], (nb, SCAN_W))
            a_im = jnp.broadcast_to(a_im_row[:, re], (nb, SCAN_W))
            h_re, h_im = h_scr[j, :, re], h_scr[j, :, im]
            for t in range(pairs):
                now = slice(t * nb, (t + 1) * nb)
                n_re = a_re * h_re - a_im * h_im + bu_scr[buf, now, re]
                n_im = a_re * h_im + a_im * h_re + bu_scr[buf, now, im]
                bu_scr[buf, now, re] = h_re
                bu_scr[buf, now, im] = h_im
                h_re, h_im = n_re, n_im
            h_scr[j, :, re] = h_re
            h_scr[j, :, im] = h_im

    def project(j):
        yy = _dot(bu_scr[j % 2].astype(_BF), cs_ref[j]) + _dot(pair_inputs(j).astype(_BF), ks_ref[j])
        for par in range(2):
            y_par = yy[:, par * LANES:(par + 1) * LANES].reshape(pairs, nb, LANES) + d_ref[j] * u_scr[j, :, par]
            yy_scr[j, :, par] = y_par

    expand(0)
    for j in range(N_STRIPS):
        if j + 1 < N_STRIPS:
            expand(j + 1)
        scan(j)
        project(j)

    halves = 2
    hs, hp, hr = steps // halves, pairs // halves, rows // halves
    ys = [jnp.concatenate([yy_scr[j, k * hp:(k + 1) * hp].reshape(hr, LANES) for j in range(N_STRIPS)], axis=1)
          for k in range(halves)]
    gs = [jax.nn.gelu(y) for y in ys]
    glu = [_dot(g.astype(_BF), wglu_ref[...]) for g in gs]
    outs = []
    for k in range(halves):
        y = gs[k] * jax.nn.sigmoid(glu[k] + bglu_ref[...]) * jax.nn.silu(z_scr[k * hr:(k + 1) * hr])
        outs.append(_dot(y.astype(_BF), wout_ref[...]).reshape(hs, nb, D_MODEL))
    for k in range(halves):
        r = (DEEPNORM_ALPHA * x3[k * hs:(k + 1) * hs] + gate[None] * outs[k]).reshape(hr, D_MODEL)
        y = _layer_norm(r, lng_ref[...], lnb_ref[...])
        if batch_major_out:
            for c in range(lane_tiles):
                xt_scr[c, k * hr:(k + 1) * hr] = y[:, c * LANES:(c + 1) * LANES]
            for b in range(nb):
                for c in range(lane_tiles):
                    y_ref[b, k * hs:(k + 1) * hs, c * LANES:(c + 1) * LANES] = (
                        xt_scr[c, pl.ds(k * hr + b, hs, stride=nb), :])
        else:
            y_ref[k * hr:(k + 1) * hr] = y

    @pl.when(i == pl.num_programs(0) - 1)
    def _():
        hout_ref[...] = h_scr[...]


def _s5_layer(x, nb, seq, ada_all, row0, h0, la, p, batch_major_in, batch_major_out):
    total = seq * nb
    rows = min(S5_ROWS, total)
    steps = rows // nb
    tb_spec = pl.BlockSpec((rows, D_MODEL), lambda i: (i, 0))
    bt_spec = pl.BlockSpec((nb, steps, D_MODEL), lambda i: (0, i, 0))
    tb_shape = jax.ShapeDtypeStruct((total, D_MODEL), _F32)
    bt_shape = jax.ShapeDtypeStruct((nb, seq, D_MODEL), _F32)
    kern = functools.partial(_s5_layer_kernel, nb=nb, steps=steps, row0=row0, batch_major_in=batch_major_in,
                             batch_major_out=batch_major_out)
    weights = [p[k] for k in ('win_a', 'bs', 'cs', 'ks', 'ar', 'ai', 'ssm_d', 'w_glu', 'b_glu', 'w_out_a', 'ln_g', 'ln_b')]
    return pl.pallas_call(
        kern,
        grid=(total // rows,),
        in_specs=[bt_spec if batch_major_in else tb_spec,
                  _layer_spec(ada_all.shape, la),
                  _layer_spec(h0.shape, la)] + [_layer_spec(w.shape, la) for w in weights],
        out_specs=[bt_spec if batch_major_out else tb_spec,
                   pl.BlockSpec((N_STRIPS, nb, STRIP_W), lambda i: (0, 0, 0))],
        out_shape=[bt_shape if batch_major_out else tb_shape,
                   jax.ShapeDtypeStruct((N_STRIPS, nb, STRIP_W), _F32)],
        scratch_shapes=[pltpu.VMEM((N_STRIPS, nb, STRIP_W), _F32),
                        pltpu.VMEM((N_STRIPS, steps // 2, 2, nb, LANES), _F32),
                        pltpu.VMEM((rows, D_MODEL), _F32),
                        pltpu.VMEM((2, rows // 2, STRIP_W), _F32),
                        pltpu.VMEM((N_STRIPS, steps // 2, 2, nb, LANES), _F32),
                        pltpu.VMEM((D_MODEL // LANES, rows, LANES), _F32)],
        compiler_params=pltpu.CompilerParams(dimension_semantics=("arbitrary",), vmem_limit_bytes=VMEM_LIMIT),
        name="s5_layer",
    )(x, ada_all, h0, *weights)


def _rope_angles(pos):
    half = HEAD_DIM // 2
    inv_freq = np.power(ROPE_THETA, -np.arange(half, dtype=np.float64) / half)
    ang = pos.astype(np.float64)[:, None] * inv_freq[None, :]
    return np.cos(ang).astype(np.float32), np.sin(ang).astype(np.float32)


def _rope_tables(pos):
    cos, sin = _rope_angles(pos)
    zero = np.zeros_like(sin)
    cos_t = np.tile(cos, (1, LANES // (HEAD_DIM // 2)))
    sin_lo = np.tile(np.concatenate([-sin, zero], axis=1), (1, LANES // HEAD_DIM))
    sin_hi = np.tile(np.concatenate([zero, sin], axis=1), (1, LANES // HEAD_DIM))
    return jnp.asarray(cos_t), jnp.asarray(sin_lo), jnp.asarray(sin_hi)


def _rope_tile(x, cos_t, sin_lo, sin_hi):
    half = HEAD_DIM // 2
    return x * cos_t + pltpu.roll(x, LANES - half, 1) * sin_lo + pltpu.roll(x, half, 1) * sin_hi


def _kv_kernel(x_ref, w_ref, cos_ref, slo_ref, shi_ref, k_ref, v_ref, kbf_ref):
    kv = _dot(x_ref[0].astype(_BF), w_ref[...])
    cos_t, sin_lo, sin_hi = cos_ref[...], slo_ref[...], shi_ref[...]
    for t in range(KV_WIDTH // LANES):
        tile = _rope_tile(kv[:, t * LANES:(t + 1) * LANES], cos_t, sin_lo, sin_hi)
        k_ref[0, :, t * LANES:(t + 1) * LANES] = tile
        kbf_ref[0, :, t * LANES:(t + 1) * LANES] = tile.astype(_BF)
    v_ref[0] = kv[:, KV_WIDTH:]


def _shared_kv(x, w_kv, tables):
    bsz, seq, _ = x.shape
    rows = min(KV_ROWS, seq)
    tab_spec = pl.BlockSpec((rows, LANES), lambda b, i: (i, 0))
    kv_spec = pl.BlockSpec((1, rows, KV_WIDTH), lambda b, i: (b, i, 0))
    out_specs = [kv_spec] * 3
    out_shape = ([jax.ShapeDtypeStruct((bsz, seq, KV_WIDTH), _F32)] * 2
                 + [jax.ShapeDtypeStruct((bsz, seq, KV_WIDTH), _BF)])
    return pl.pallas_call(
        _kv_kernel,
        grid=(bsz, seq // rows),
        in_specs=[pl.BlockSpec((1, rows, D_MODEL), lambda b, i: (b, i, 0)),
                  _const_spec((D_MODEL, 2 * KV_WIDTH)), tab_spec, tab_spec, tab_spec],
        out_specs=out_specs, out_shape=out_shape,
        compiler_params=pltpu.CompilerParams(dimension_semantics=("arbitrary", "arbitrary"),
                                             vmem_limit_bytes=VMEM_LIMIT),
        name="shared_kv",
    )(x, w_kv, *tables)


def _reduce_rows(x, pair_op, final_op):
    sub = 8
    parts = [x[i * sub:(i + 1) * sub] for i in range(x.shape[0] // sub)]
    while len(parts) > 1:
        parts = [pair_op(parts[i], parts[i + 1]) for i in range(0, len(parts), 2)]
    return final_op(parts[0], axis=0, keepdims=True)


def _attn_layer_kernel(sinks_ref, x_ref, ada_ref, win_ref, wout_ref, cos_ref, sin_ref,
                       lng_ref, lnb_ref, *refs, bb, rb, row0, layer_b, banded, make_kv):
    rows = bb * rb
    n_pairs = rows // LANES
    pairs_per_stream = rb // LANES
    if make_kv:
        (wkv_ref, cosr_ref, slo_ref, shi_ref, y_ref, kcache_ref, vcache_ref, kbf_ref, vtout_ref,
         hm_scr, blk_scr, qt_scr, gt_scr, ot_scr, ob_scr, acc_scr, cap_scr, k_src, vt_src) = refs
        step = pl.program_id(1)
        kv_new = _dot(x_ref[0].astype(_BF), wkv_ref[...])
        first_row = pl.multiple_of(step * rb, rb)
        for t in range(KV_WIDTH // LANES):
            tile = _rope_tile(kv_new[:, t * LANES:(t + 1) * LANES], cosr_ref[...], slo_ref[...], shi_ref[...])
            kcache_ref[0, :, t * LANES:(t + 1) * LANES] = tile[rb - WINDOW:]
            kbf_ref[0, :, t * LANES:(t + 1) * LANES] = tile.astype(_BF)
            k_src[0, pl.ds(first_row, rb), t * LANES:(t + 1) * LANES] = tile.astype(_BF)
        v_new = kv_new[:, KV_WIDTH:]
        vcache_ref[0] = v_new[rb - WINDOW:]
        for blk in range(rb // LANES):
            vt_blk = v_new[blk * LANES:(blk + 1) * LANES].T.astype(_BF)
            vtout_ref[0, blk] = vt_blk
            vt_src[0, step * (rb // LANES) + blk] = vt_blk
    else:
        (k_src, vt_src, y_ref, hm_scr, blk_scr, qt_scr, gt_scr, ot_scr, ob_scr, acc_scr, cap_scr) = refs

    def ada_row(k, b):
        if banded:
            return ada_ref[k, pl.ds(row0 + pl.program_id(0) * bb + b, 1), :]
        return ada_ref[k, row0 + b:row0 + b + 1, :]

    for b in range(bb):
        hm_scr[b * rb:(b + 1) * rb] = (x_ref[b] * (1.0 + ada_row(1, b)) + ada_row(0, b)).astype(_BF)
    cos_t, sin_t = cos_ref[...], sin_ref[...]
    half = HEAD_DIM // 2
    qk_scale = HEAD_DIM ** -0.5 * LOG2_E
    kt = IN_PIECE

    def in_piece(c, k):
        part = _dot(hm_scr[:, k * kt:(k + 1) * kt], win_ref[k * kt:(k + 1) * kt, c * HEAD_TILE:(c + 1) * HEAD_TILE])
        if k == 0:
            blk_scr[c // N_KV_HEADS] = part
        else:
            blk_scr[c // N_KV_HEADS] += part

    def in_finish(c):
        t = blk_scr[c // N_KV_HEADS].T
        if c < N_KV_HEADS:
            for j in range(Q_PER_KV):
                lo = c * HEAD_TILE + j * HEAD_DIM
                x1, x2 = t[j * HEAD_DIM:j * HEAD_DIM + half], t[j * HEAD_DIM + half:(j + 1) * HEAD_DIM]
                qt_scr[lo:lo + half] = ((x1 * cos_t - x2 * sin_t) * qk_scale).astype(_BF)
                qt_scr[lo + half:lo + HEAD_DIM] = ((x2 * cos_t + x1 * sin_t) * qk_scale).astype(_BF)
        else:
            lo = (c - N_KV_HEADS) * HEAD_TILE
            gt_scr[lo:lo + HEAD_TILE] = jax.nn.silu(t)

    def out_prepare(kv):
        ob_scr[...] = ot_scr[kv * HEAD_TILE:(kv + 1) * HEAD_TILE].T.astype(_BF)

    def out_piece(kv, n):
        part = _dot(ob_scr[...], wout_ref[kv * HEAD_TILE:(kv + 1) * HEAD_TILE, n * HEAD_TILE:(n + 1) * HEAD_TILE])
        if kv == 0:
            acc_scr[:, n * HEAD_TILE:(n + 1) * HEAD_TILE] = part
        else:
            acc_scr[:, n * HEAD_TILE:(n + 1) * HEAD_TILE] += part

    key_blk = lax.broadcasted_iota(jnp.int32, (PAIR_KEYS, LANES), 0) // CHUNK
    qry_blk = lax.broadcasted_iota(jnp.int32, (PAIR_KEYS, LANES), 1) // CHUNK
    no_q = jnp.zeros((HEAD_DIM, LANES), _BF)
    pairs = []
    for a in range(n_pairs):
        if banded:
            qc0 = pl.program_id(1) * (rb // CHUNK) + 2 * (a % pairs_per_stream)
            kc0 = jnp.maximum(qc0 - WINDOW_CHUNKS, 0)
            delta0 = kc0 - qc0
            kstart = pl.multiple_of(kc0 * CHUNK, LANES)
            vblk = kc0 // 2
        else:
            delta0, kstart, vblk = -WINDOW_CHUNKS, 0, 0
        rel = delta0 + key_blk - qry_blk
        cap_scr[a] = jnp.where((rel >= -WINDOW_CHUNKS) & (rel <= 0), MASK_PASS, NEG_INF)
        pairs.append((a // pairs_per_stream, kstart, vblk))

    def unit_heads(kv, gp):
        return (kv * Q_PER_KV + 2 * gp, kv * Q_PER_KV + 2 * gp + 1)

    def scores(a, kv, gp):
        lanes = slice(a * LANES, (a + 1) * LANES)
        stream, kstart, _ = pairs[a]
        k_tile = k_src[stream, pl.ds(kstart, PAIR_KEYS), (kv // 2) * LANES:(kv // 2 + 1) * LANES]
        w_parts = []
        for h in unit_heads(kv, gp):
            q_h = qt_scr[h * HEAD_DIM:(h + 1) * HEAD_DIM, lanes]
            w_parts.append(jnp.concatenate([q_h, no_q] if kv % 2 == 0 else [no_q, q_h], axis=0))
        return _dot(k_tile, jnp.concatenate(w_parts, axis=1))

    def softmax_values(a, kv, gp, s):
        e_parts, inv_den = [], []
        for u, h in enumerate(unit_heads(kv, gp)):
            s_h = jnp.minimum(s[:, u * LANES:(u + 1) * LANES], cap_scr[a])
            sink = sinks_ref[layer_b, h] * LOG2_E
            m = jnp.maximum(_reduce_rows(s_h, jnp.maximum, jnp.max), sink)
            e = jnp.exp2(s_h - m)
            inv_den.append(1.0 / (_reduce_rows(e, jnp.add, jnp.sum) + jnp.exp2(sink - m)))
            e_parts.append(e.astype(_BF))
        stream, _, vblk = pairs[a]
        vt_kv = jnp.concatenate([vt_src[stream, vblk, kv * HEAD_DIM:(kv + 1) * HEAD_DIM, :],
                                 vt_src[stream, vblk + 1, kv * HEAD_DIM:(kv + 1) * HEAD_DIM, :]], axis=1)
        return _dot(vt_kv, jnp.concatenate(e_parts, axis=1)), inv_den

    def finish(a, kv, gp, ot, inv_den):
        lanes = slice(a * LANES, (a + 1) * LANES)
        for u, h in enumerate(unit_heads(kv, gp)):
            rows_h = slice(h * HEAD_DIM, (h + 1) * HEAD_DIM)
            ot_scr[rows_h, lanes] = ot[:, u * LANES:(u + 1) * LANES] * inv_den[u] * gt_scr[rows_h, lanes]

    n_k = D_MODEL // kt

    def in_block(c):
        return [functools.partial(in_piece, c, k) for k in range(n_k)] + [functools.partial(in_finish, c)]

    for item in in_block(0) + in_block(N_KV_HEADS):
        item()
    slots = []
    for kv in range(N_KV_HEADS):
        units = [(a, kv, gp) for a in range(n_pairs) for gp in range(Q_PER_KV // 2)]
        work = [[] for _ in units]
        ahead, behind = [], []
        if kv + 1 < N_KV_HEADS:
            ahead = in_block(kv + 1) + in_block(kv + 1 + N_KV_HEADS)
        if kv > 0:
            behind = [functools.partial(out_prepare, kv - 1)] + [
                functools.partial(out_piece, kv - 1, n) for n in range(D_MODEL // HEAD_TILE)]
        for m, item in enumerate(ahead):
            work[m * len(units) // len(ahead)].append(item)
        for m, item in enumerate(behind):
            work[1 + m * (len(units) - 1) // len(behind)].append(item)
        slots += list(zip(units, work))
    pending = [scores(*slots[k][0]) for k in range(SCORE_AHEAD)]
    unfinished = None
    for i, (unit, work) in enumerate(slots):
        if i + SCORE_AHEAD < len(slots):
            pending.append(scores(*slots[i + SCORE_AHEAD][0]))
        s_cur = pending.pop(0)
        for item in work:
            item()
        ot, inv_den = softmax_values(*unit, s_cur)
        if unfinished is not None:
            finish(*unfinished)
        unfinished = (*unit, ot, inv_den)
    finish(*unfinished)
    out_prepare(N_KV_HEADS - 1)
    half_rows = rows // 2
    seg = min(rb, half_rows)
    w_last = wout_ref[(N_KV_HEADS - 1) * HEAD_TILE:N_KV_HEADS * HEAD_TILE, :]
    outs = [acc_scr[k * half_rows:(k + 1) * half_rows] + _dot(ob_scr[k * half_rows:(k + 1) * half_rows], w_last)
            for k in range(2)]
    for k in range(2):
        for s in range(half_rows // seg):
            r0 = k * half_rows + s * seg
            b, q0 = r0 // rb, r0 % rb
            r = DEEPNORM_ALPHA * x_ref[b, q0:q0 + seg] + ada_row(2, b) * outs[k][s * seg:(s + 1) * seg]
            y_ref[b, q0:q0 + seg] = _layer_norm(r, lng_ref[...], lnb_ref[...])


def _attn_layer(x, ada_all, row0, lb, p, tables_t, banded, bb, rb, kv=None, kv_maker=None):
    bsz, seq, _ = x.shape
    rows = bb * rb
    layer = N_A_LAYERS + lb
    row_spec = pl.BlockSpec((bb, rb, D_MODEL), lambda b, i: (b, i, 0))
    tab_spec = pl.BlockSpec((HEAD_DIM // 2, rows), lambda b, i: (0, i))
    y_shape = jax.ShapeDtypeStruct((bsz, seq, D_MODEL), _F32)
    kern = functools.partial(_attn_layer_kernel, bb=bb, rb=rb, row0=row0, layer_b=lb, banded=banded,
                             make_kv=kv is None)
    if kv is None:
        assert bb == 1
        w_kv, row_tables = kv_maker
        rtab_spec = pl.BlockSpec((rb, LANES), lambda b, i: (i, 0))
        kv_in, kv_specs = (w_kv, *row_tables), [_const_spec((D_MODEL, 2 * KV_WIDTH))] + [rtab_spec] * 3
        cache_spec = pl.BlockSpec((1, WINDOW, KV_WIDTH), lambda b, i: (b, 0, 0))
        out_specs = [row_spec, cache_spec, cache_spec,
                     pl.BlockSpec((1, rb, KV_WIDTH), lambda b, i: (b, i, 0)),
                     pl.BlockSpec((1, rb // LANES, KV_WIDTH, LANES), lambda b, i: (b, i, 0, 0))]
        out_shape = [y_shape] + [jax.ShapeDtypeStruct((bsz, WINDOW, KV_WIDTH), _F32)] * 2 + [
            jax.ShapeDtypeStruct((bsz, seq, KV_WIDTH), _BF),
            jax.ShapeDtypeStruct((bsz, seq // LANES, KV_WIDTH, LANES), _BF)]
        kv_scratch = [pltpu.VMEM((1, seq, KV_WIDTH), _BF), pltpu.VMEM((1, seq // LANES, KV_WIDTH, LANES), _BF)]
    else:
        keys = kv[0].shape[1]
        kv_in, kv_specs = kv, [pl.BlockSpec((bb, keys, KV_WIDTH), lambda b, i: (b, 0, 0)),
                               pl.BlockSpec((bb, keys // LANES, KV_WIDTH, LANES), lambda b, i: (b, 0, 0, 0))]
        out_specs, out_shape, kv_scratch = row_spec, y_shape, []
    return pl.pallas_call(
        kern,
        grid=(bsz // bb, seq // rb),
        in_specs=[pl.BlockSpec(memory_space=pltpu.SMEM),
                  row_spec, _layer_spec(ada_all.shape, layer),
                  _layer_spec(p['win_b'].shape, lb), _layer_spec(p['w_out_b'].shape, lb),
                  tab_spec, tab_spec,
                  _layer_spec(p['ln_g'].shape, layer), _layer_spec(p['ln_b'].shape, layer)] + kv_specs,
        out_specs=out_specs,
        out_shape=out_shape,
        scratch_shapes=[pltpu.VMEM((rows, D_MODEL), _BF),
                        pltpu.VMEM((2, rows, HEAD_TILE), _F32),
                        pltpu.VMEM((D_MODEL, rows), _BF),
                        pltpu.VMEM((D_MODEL, rows), _F32),
                        pltpu.VMEM((D_MODEL, rows), _F32),
                        pltpu.VMEM((rows, HEAD_TILE), _BF),
                        pltpu.VMEM((rows, D_MODEL), _F32),
                        pltpu.VMEM((rows // LANES, PAIR_KEYS, LANES), _F32)]
        + kv_scratch,
        compiler_params=pltpu.CompilerParams(dimension_semantics=("arbitrary", "arbitrary"),
                                             vmem_limit_bytes=VMEM_LIMIT),
        name="attn_layer",
    )(p['attn_sinks'], x, ada_all, p['win_b'], p['w_out_b'], *tables_t, p['ln_g'], p['ln_b'], *kv_in)


def _run_trunk(x, ada_all, row0, pos, h0, cache_k, cache_v, p):
    bsz, seq, _ = x.shape
    states = []
    for la in range(N_A_LAYERS):
        x, h_last = _s5_layer(x, bsz, seq, ada_all, row0, h0, la, p,
                              batch_major_in=la == 0, batch_major_out=la == N_A_LAYERS - 1)
        states.append(_strips_to_state(h_last))
    banded = cache_k is None
    row_tables = _rope_tables(pos)
    if banded:
        kv, kv_maker, q_pos = None, (p['w_kv'], row_tables), pos
        bb, rb = 1, ATTN_ROWS
    else:
        k_new, v_new, k_bf = _shared_kv(x, p['w_kv'], row_tables)
        pad = LANES - seq
        flat = lambda c: c.reshape(bsz, c.shape[1], KV_WIDTH)
        k_bf = jnp.concatenate([flat(cache_k).astype(_BF), k_bf, jnp.zeros((bsz, pad, KV_WIDTH), _BF)], axis=1)
        v_all = jnp.concatenate([flat(cache_v), v_new, jnp.zeros((bsz, pad, KV_WIDTH), _F32)], axis=1)
        vt = v_all.reshape(bsz, PAIR_KEYS // LANES, LANES, KV_WIDTH).transpose(0, 1, 3, 2).astype(_BF)
        kv, kv_maker = (k_bf, vt), None
        x = jnp.pad(x, ((0, 0), (0, pad), (0, 0)))
        q_pos = np.tile(pos[0] + np.arange(LANES), bsz)
        bb, rb = bsz, LANES
    cos, sin = _rope_angles(q_pos)
    tables_t = (jnp.asarray(cos.T), jnp.asarray(sin.T))
    for lb in range(N_B_LAYERS):
        out = _attn_layer(x, ada_all, row0, lb, p, tables_t, banded=banded, bb=bb, rb=rb, kv=kv, kv_maker=kv_maker)
        if kv is None:
            x, k_new, v_new, k_bf, vt = out
            kv = (k_bf, vt)
        else:
            x = out
    k4 = k_new.reshape(bsz, k_new.shape[1], N_KV_HEADS, HEAD_DIM)
    v4 = v_new.reshape(bsz, v_new.shape[1], N_KV_HEADS, HEAD_DIM)
    return x[:, :seq], jnp.stack(states), k4, v4


def kernel(x_prompt, x_sample, state_ssm, cache_k, cache_v, c_prompt, c_sample, w_ada, b_ada, ln_g, ln_b, w_in_a,
           ssm_a_re, ssm_a_im, ssm_b_re, ssm_b_im, ssm_c_re, ssm_c_im, ssm_d, ssm_log_dt, w_glu, b_glu, w_out_a,
           w_kv, w_in_b, attn_sinks, w_out_b):
    n_prompt, n_sample = x_prompt.shape[0], x_sample.shape[0]
    ada_all = _ada_params(jnp.concatenate([c_prompt, c_sample], axis=0), w_ada, b_ada)
    bs, cs, ks, ar, ai = _s5_matrices(
        _s5_prep(ssm_a_re, ssm_a_im, ssm_log_dt, ssm_b_re, ssm_b_im, ssm_c_re, ssm_c_im))
    p = dict(win_a=w_in_a.astype(_BF), bs=bs, cs=cs, ks=ks, ar=ar, ai=ai,
             ssm_d=ssm_d.reshape(N_A_LAYERS, N_STRIPS, 1, LANES), w_glu=w_glu.astype(_BF),
             b_glu=b_glu.reshape(N_A_LAYERS, 1, D_MODEL), w_out_a=w_out_a.astype(_BF),
             ln_g=ln_g.reshape(DEPTH, 1, D_MODEL), ln_b=ln_b.reshape(DEPTH, 1, D_MODEL), w_kv=w_kv.astype(_BF),
             win_b=w_in_b.astype(_BF), w_out_b=w_out_b.astype(_BF), attn_sinks=attn_sinks)

    pos_prompt = np.arange(x_prompt.shape[1])
    pos_sample = PAST_LEN + np.arange(x_sample.shape[1])
    h0_prompt = jnp.zeros((N_A_LAYERS, N_STRIPS, n_prompt, STRIP_W), _F32)
    h0_sample = jnp.stack([_state_to_strips(state_ssm[la]) for la in range(N_A_LAYERS)])

    y_p, ssm_p, k_p, v_p = _run_trunk(x_prompt, ada_all, 0, pos_prompt, h0_prompt, None, None, p)
    y_s, ssm_s, k_s, v_s = _run_trunk(x_sample, ada_all, n_prompt, pos_sample, h0_sample, cache_k, cache_v, p)
    rows = min(WINDOW, x_prompt.shape[1])
    return (y_p, y_s, ssm_p, k_p[:, -rows:], v_p[:, -rows:], ssm_s, k_s, v_s)
```

```python
import functools
import math

import jax
import jax.numpy as jnp
import numpy as np
from jax import lax
from jax.experimental import pallas as pl
from jax.experimental.pallas import tpu as pltpu

D_MODEL = 1024
DEPTH = 4
CHUNK = 64
N_A_LAYERS = 2
N_B_LAYERS = 2
SSM_GROUP_CH = 16
SSM_GROUPS = 64
SSM_STATE = 64
HEAD_DIM = 64
N_HEADS = 16
N_KV_HEADS = 4
Q_PER_KV = 4
KV_WIDTH = N_KV_HEADS * HEAD_DIM
WINDOW = 128
WINDOW_CHUNKS = WINDOW // CHUNK
PAST_LEN = 1024
ROPE_THETA = 10000.0
NEG_INF = -1e30
DEEPNORM_ALPHA = (2.0 * DEPTH) ** 0.25
LN_EPS = 1e-5

LANES = 128
STRIP_GROUPS = LANES // SSM_GROUP_CH
N_STRIPS = SSM_GROUPS // STRIP_GROUPS
STRIP_HALF = STRIP_GROUPS * SSM_STATE
STRIP_W = 2 * STRIP_HALF
SCAN_W = 256
S5_ROWS = 512
KEYS = (WINDOW_CHUNKS + 1) * CHUNK
PAIR_KEYS = KEYS + CHUNK
ATTN_ROWS = 1024
HEAD_TILE = Q_PER_KV * HEAD_DIM
IN_PIECE = 512
SCORE_AHEAD = 3
KV_ROWS = 512
MASK_PASS = 3.0e38
LOG2_E = math.log2(math.e)
VMEM_LIMIT = 56 * 1024 * 1024

_BF = jnp.bfloat16
_F32 = jnp.float32


def _dot(a, b):
    return jnp.dot(a, b, preferred_element_type=_F32)


def _const_spec(shape):
    nd = len(shape)
    return pl.BlockSpec(shape, lambda *_: (0,) * nd, pipeline_mode=pl.Buffered(1))


def _layer_spec(shape, layer):
    nd = len(shape)
    return pl.BlockSpec((None,) + tuple(shape[1:]), lambda *_: (layer,) + (0,) * (nd - 1),
                        pipeline_mode=pl.Buffered(1))


def _layer_norm(r, g, b):
    mu = jnp.mean(r, axis=-1, keepdims=True)
    d = r - mu
    var = jnp.mean(d * d, axis=-1, keepdims=True)
    return d * lax.rsqrt(var + LN_EPS) * g + b


def _ada_kernel(c_ref, w_ref, b_ref, o_ref):
    c = c_ref[...]
    o_ref[0, 0] = _dot(jax.nn.silu(c).astype(_BF), w_ref[0].astype(_BF)) + b_ref[0]


def _ada_params(c_all, w_ada, b_ada):
    n = c_all.shape[0]
    return pl.pallas_call(
        _ada_kernel,
        grid=(DEPTH, 3),
        in_specs=[pl.BlockSpec((n, D_MODEL), lambda l, j: (0, 0)),
                  pl.BlockSpec((1, D_MODEL, D_MODEL), lambda l, j: (l, 0, j)),
                  pl.BlockSpec((1, 1, D_MODEL), lambda l, j: (l, 0, j))],
        out_specs=pl.BlockSpec((1, 1, n, D_MODEL), lambda l, j: (l, j, 0, 0)),
        out_shape=jax.ShapeDtypeStruct((DEPTH, 3, n, D_MODEL), _F32),
        name="ada_params",
    )(c_all, w_ada, b_ada.reshape(DEPTH, 1, 3 * D_MODEL))


def _cmul(x_re, x_im, y_re, y_im):
    return x_re * y_re - x_im * y_im, x_re * y_im + x_im * y_re


def _zoh_a(a_re, a_im, log_dt):
    dt = jnp.exp(log_dt)
    mag = jnp.exp(a_re * dt)
    return mag * jnp.cos(a_im * dt), mag * jnp.sin(a_im * dt)


def _s5_prep_kernel(are_ref, aim_ref, ldt_ref, bre_ref, bim_ref, aret_ref, aimt_ref, ldtt_ref, cre_ref, cim_ref,
                    a2r_ref, a2i_ref, bbr_ref, bbi_ref, abr_ref, abi_ref, car_ref, cai_ref, ca2r_ref, ca2i_ref):
    a_re, a_im = are_ref[0], aim_ref[0]
    ab_re, ab_im = _zoh_a(a_re, a_im, ldt_ref[0])
    nr, ni = ab_re - 1.0, ab_im
    den = a_re * a_re + a_im * a_im
    f_re = (nr * a_re + ni * a_im) / den
    f_im = (ni * a_re - nr * a_im) / den
    bb_re, bb_im = _cmul(f_re, f_im, bre_ref[0], bim_ref[0])
    a2r_ref[0], a2i_ref[0] = _cmul(ab_re, ab_im, ab_re, ab_im)
    bbr_ref[0], bbi_ref[0] = bb_re, bb_im
    abr_ref[0], abi_ref[0] = _cmul(ab_re, ab_im, bb_re, bb_im)
    at_re, at_im = _zoh_a(aret_ref[0], aimt_ref[0], ldtt_ref[0])
    ca_re, ca_im = _cmul(cre_ref[0], cim_ref[0], at_re, at_im)
    car_ref[0], cai_ref[0] = ca_re, ca_im
    ca2r_ref[0], ca2i_ref[0] = _cmul(ca_re, ca_im, at_re, at_im)


def _s5_lag_kernel(cr_ref, ci_ref, car_ref, cai_ref, bbr_ref, bbi_ref, k0_ref, k1_ref):
    def re_prod(x_re, x_im):
        dot = lambda x, y: jnp.einsum('gcp,gpd->gcd', x, y, precision=lax.Precision.HIGHEST,
                                      preferred_element_type=_F32)
        return dot(x_re, bbr_ref[0]) - dot(x_im, bbi_ref[0])
    k0_ref[0] = re_prod(cr_ref[0], ci_ref[0])
    k1_ref[0] = re_prod(car_ref[0], cai_ref[0])


def _s5_prep(a_re, a_im, log_dt, b_re, b_im, c_re, c_im):
    nl, w = N_A_LAYERS, SSM_STATE * SSM_GROUP_CH
    flat = lambda a: a.reshape(nl, SSM_GROUPS, w)
    rep = lambda a: jnp.repeat(a, SSM_GROUP_CH, axis=-1)
    til = lambda a: jnp.tile(a, (1, 1, SSM_GROUP_CH))
    ldt = log_dt[:, :, None]
    spec = pl.BlockSpec((1, SSM_GROUPS, w), lambda l: (l, 0, 0))
    shp = jax.ShapeDtypeStruct((nl, SSM_GROUPS, w), _F32)
    a2_re, a2_im, bb_re, bb_im, abb_re, abb_im, ca_re, ca_im, ca2_re, ca2_im = pl.pallas_call(
        _s5_prep_kernel, grid=(nl,), in_specs=[spec] * 10, out_specs=[spec] * 10, out_shape=[shp] * 10, name="s5_prep",
    )(rep(a_re), rep(a_im), rep(jnp.broadcast_to(ldt, a_re.shape)), flat(b_re), flat(b_im),
      til(a_re), til(a_im), til(jnp.broadcast_to(ldt, a_re.shape)), flat(c_re), flat(c_im))
    pc = (nl, SSM_GROUPS, SSM_STATE, SSM_GROUP_CH)
    cp = (nl, SSM_GROUPS, SSM_GROUP_CH, SSM_STATE)
    cp_spec = pl.BlockSpec((1,) + cp[1:], lambda l: (l, 0, 0, 0))
    pc_spec = pl.BlockSpec((1,) + pc[1:], lambda l: (l, 0, 0, 0))
    kk = (nl, SSM_GROUPS, SSM_GROUP_CH, SSM_GROUP_CH)
    kk_spec = pl.BlockSpec((1,) + kk[1:], lambda l: (l, 0, 0, 0))
    k0, k1 = pl.pallas_call(
        _s5_lag_kernel, grid=(nl,), in_specs=[cp_spec] * 4 + [pc_spec] * 2, out_specs=[kk_spec] * 2,
        out_shape=[jax.ShapeDtypeStruct(kk, _F32)] * 2, name="s5_lag",
    )(c_re, c_im, ca_re.reshape(cp), ca_im.reshape(cp), bb_re.reshape(pc), bb_im.reshape(pc))
    return dict(a2_re=a2_re.reshape(pc)[..., 0], a2_im=a2_im.reshape(pc)[..., 0],
                bb_re=bb_re.reshape(pc), bb_im=bb_im.reshape(pc), abb_re=abb_re.reshape(pc), abb_im=abb_im.reshape(pc),
                ca_re=ca_re.reshape(cp), ca_im=ca_im.reshape(cp), ca2_re=ca2_re.reshape(cp), ca2_im=ca2_im.reshape(cp),
                k0=k0, k1=k1)


def _s5_matrices(q):
    same = jnp.eye(STRIP_GROUPS, dtype=bool)
    nl = N_A_LAYERS
    zero = jnp.zeros((), _BF)
    state_cols = np.arange(STRIP_W)
    spread = np.equal.outer(np.arange(2 * SSM_STATE),
                            (state_cols // STRIP_HALF) * SSM_STATE + state_cols % SSM_STATE)
    chan_rows = np.arange(2 * LANES)
    diag = np.equal.outer(chan_rows % LANES // SSM_GROUP_CH, state_cols % STRIP_HALF // SSM_STATE)
    spread_bf = jnp.asarray(spread, _BF)

    def compact(parts, order):
        x = jnp.stack([jnp.stack(ri, axis=0) for ri in parts], axis=0).astype(_BF)
        x = x.reshape(2, 2, nl, N_STRIPS, STRIP_GROUPS, x.shape[-2], x.shape[-1])
        return x.transpose(order)

    b_c = compact([(q['abb_re'], q['abb_im']), (q['bb_re'], q['bb_im'])], (2, 3, 0, 4, 6, 1, 5))
    b_c = b_c.reshape(nl, N_STRIPS, 2 * LANES, 2 * SSM_STATE)
    bs2 = jnp.where(diag, jnp.einsum('ljab,bn->ljan', b_c, spread_bf, preferred_element_type=_BF), zero)
    c_c = compact([(q['ca_re'], -q['ca_im']), (q['ca2_re'], -q['ca2_im'])], (2, 3, 1, 6, 0, 4, 5))
    c_c = c_c.reshape(nl, N_STRIPS, 2 * SSM_STATE, 2 * LANES)
    cs2 = jnp.where(diag.T, jnp.einsum('nb,ljba->ljna', spread_bf.T, c_c, preferred_element_type=_BF), zero)

    def lag(k):
        kt = k.astype(_BF).transpose(0, 1, 3, 2).reshape(nl, N_STRIPS, STRIP_GROUPS, SSM_GROUP_CH, SSM_GROUP_CH)
        placed = jnp.where(same[None, None, :, None, :, None], kt[:, :, :, :, None, :], zero)
        return placed.reshape(nl, N_STRIPS, LANES, LANES)

    k0, k1 = lag(q['k0']), lag(q['k1'])
    ks = jnp.concatenate([jnp.concatenate([k0, k1], axis=3),
                          jnp.concatenate([jnp.zeros_like(k0), k0], axis=3)], axis=2)
    a2r = q['a2_re'].reshape(nl, N_STRIPS, 1, STRIP_HALF)
    a2i = q['a2_im'].reshape(nl, N_STRIPS, 1, STRIP_HALF)
    return bs2, cs2, ks, a2r, a2i


def _state_to_strips(state):
    nb = state.shape[0]
    s = state.reshape(nb, N_STRIPS, STRIP_GROUPS, SSM_STATE, 2)
    return s.transpose(1, 0, 4, 2, 3).reshape(N_STRIPS, nb, STRIP_W)


def _strips_to_state(h):
    nb = h.shape[1]
    s = h.reshape(N_STRIPS, nb, 2, STRIP_GROUPS, SSM_STATE)
    return s.transpose(1, 0, 3, 4, 2).reshape(nb, SSM_GROUPS, SSM_STATE, 2)


def _s5_layer_kernel(x_ref, ada_ref, h0_ref, win_ref, bs_ref, cs_ref, ks_ref, ar_ref, ai_ref,
                     d_ref, wglu_ref, bglu_ref, wout_ref, lng_ref, lnb_ref,
                     y_ref, hout_ref,
                     h_scr, u_scr, z_scr, bu_scr, yy_scr, xt_scr, *, nb, steps, row0, batch_major_in, batch_major_out):
    rows = nb * steps
    shift, scale, gate = (ada_ref[k, row0:row0 + nb, :] for k in range(3))
    i = pl.program_id(0)

    @pl.when(i == 0)
    def _():
        h_scr[...] = h0_ref[...]

    lane_tiles = D_MODEL // LANES
    if batch_major_in:
        for b in range(nb):
            for c in range(lane_tiles):
                xt_scr[c, pl.ds(b, steps, stride=nb), :] = x_ref[b, :, c * LANES:(c + 1) * LANES]
        x_rows = jnp.concatenate([xt_scr[c] for c in range(lane_tiles)], axis=1)
    else:
        x_rows = x_ref[...]
    x3 = x_rows.reshape(steps, nb, D_MODEL)
    hmod = (x3 * (1.0 + scale)[None] + shift[None]).reshape(rows, D_MODEL)
    uz = _dot(hmod.astype(_BF), win_ref[...])
    pairs = steps // 2
    prow = pairs * nb
    for j in range(N_STRIPS):
        u_scr[j] = uz[:, j * LANES:(j + 1) * LANES].reshape(pairs, 2, nb, LANES)
    z_scr[...] = uz[:, D_MODEL:]

    def pair_inputs(j):
        return jnp.concatenate([u_scr[j, :, 0].reshape(prow, LANES), u_scr[j, :, 1].reshape(prow, LANES)], axis=1)

    def expand(j):
        bu_scr[j % 2] = _dot(pair_inputs(j).astype(_BF), bs_ref[j])

    def scan(j):
        buf = j % 2
        a_re_row, a_im_row = ar_ref[j], ai_ref[j]
        for s in range(STRIP_HALF // SCAN_W):
            re = slice(s * SCAN_W, (s + 1) * SCAN_W)
            im = slice(STRIP_HALF + s * SCAN_W, STRIP_HALF + (s + 1) * SCAN_W)
            a_re = jnp.broadcast_to(a_re_row[:, re], (nb, SCAN_W))
            a_im = jnp.broadcast_to(a_im_row[:, re], (nb, SCAN_W))
            h_re, h_im = h_scr[j, :, re], h_scr[j, :, im]
            for t in range(pairs):
                now = slice(t * nb, (t + 1) * nb)
                n_re = a_re * h_re - a_im * h_im + bu_scr[buf, now, re]
                n_im = a_re * h_im + a_im * h_re + bu_scr[buf, now, im]
                bu_scr[buf, now, re] = h_re
                bu_scr[buf, now, im] = h_im
                h_re, h_im = n_re, n_im
            h_scr[j, :, re] = h_re
            h_scr[j, :, im] = h_im

    def project(j):
        yy = _dot(bu_scr[j % 2].astype(_BF), cs_ref[j]) + _dot(pair_inputs(j).astype(_BF), ks_ref[j])
        for par in range(2):
            y_par = yy[:, par * LANES:(par + 1) * LANES].reshape(pairs, nb, LANES) + d_ref[j] * u_scr[j, :, par]
            yy_scr[j, :, par] = y_par

    expand(0)
    for j in range(N_STRIPS):
        if j + 1 < N_STRIPS:
            expand(j + 1)
        scan(j)
        project(j)

    halves = 2
    hs, hp, hr = steps // halves, pairs // halves, rows // halves
    ys = [jnp.concatenate([yy_scr[j, k * hp:(k + 1) * hp].reshape(hr, LANES) for j in range(N_STRIPS)], axis=1)
          for k in range(halves)]
    gs = [jax.nn.gelu(y) for y in ys]
    glu = [_dot(g.astype(_BF), wglu_ref[...]) for g in gs]
    outs = []
    for k in range(halves):
        y = gs[k] * jax.nn.sigmoid(glu[k] + bglu_ref[...]) * jax.nn.silu(z_scr[k * hr:(k + 1) * hr])
        outs.append(_dot(y.astype(_BF), wout_ref[...]).reshape(hs, nb, D_MODEL))
    for k in range(halves):
        r = (DEEPNORM_ALPHA * x3[k * hs:(k + 1) * hs] + gate[None] * outs[k]).reshape(hr, D_MODEL)
        y = _layer_norm(r, lng_ref[...], lnb_ref[...])
        if batch_major_out:
            for c in range(lane_tiles):
                xt_scr[c, k * hr:(k + 1) * hr] = y[:, c * LANES:(c + 1) * LANES]
            for b in range(nb):
                for c in range(lane_tiles):
                    y_ref[b, k * hs:(k + 1) * hs, c * LANES:(c + 1) * LANES] = (
                        xt_scr[c, pl.ds(k * hr + b, hs, stride=nb), :])
        else:
            y_ref[k * hr:(k + 1) * hr] = y

    @pl.when(i == pl.num_programs(0) - 1)
    def _():
        hout_ref[...] = h_scr[...]


def _s5_layer(x, nb, seq, ada_all, row0, h0, la, p, batch_major_in, batch_major_out):
    total = seq * nb
    rows = min(S5_ROWS, total)
    steps = rows // nb
    tb_spec = pl.BlockSpec((rows, D_MODEL), lambda i: (i, 0))
    bt_spec = pl.BlockSpec((nb, steps, D_MODEL), lambda i: (0, i, 0))
    tb_shape = jax.ShapeDtypeStruct((total, D_MODEL), _F32)
    bt_shape = jax.ShapeDtypeStruct((nb, seq, D_MODEL), _F32)
    kern = functools.partial(_s5_layer_kernel, nb=nb, steps=steps, row0=row0, batch_major_in=batch_major_in,
                             batch_major_out=batch_major_out)
    weights = [p[k] for k in ('win_a', 'bs', 'cs', 'ks', 'ar', 'ai', 'ssm_d', 'w_glu', 'b_glu', 'w_out_a', 'ln_g', 'ln_b')]
    return pl.pallas_call(
        kern,
        grid=(total // rows,),
        in_specs=[bt_spec if batch_major_in else tb_spec,
                  _layer_spec(ada_all.shape, la),
                  _layer_spec(h0.shape, la)] + [_layer_spec(w.shape, la) for w in weights],
        out_specs=[bt_spec if batch_major_out else tb_spec,
                   pl.BlockSpec((N_STRIPS, nb, STRIP_W), lambda i: (0, 0, 0))],
        out_shape=[bt_shape if batch_major_out else tb_shape,
                   jax.ShapeDtypeStruct((N_STRIPS, nb, STRIP_W), _F32)],
        scratch_shapes=[pltpu.VMEM((N_STRIPS, nb, STRIP_W), _F32),
                        pltpu.VMEM((N_STRIPS, steps // 2, 2, nb, LANES), _F32),
                        pltpu.VMEM((rows, D_MODEL), _F32),
                        pltpu.VMEM((2, rows // 2, STRIP_W), _F32),
                        pltpu.VMEM((N_STRIPS, steps // 2, 2, nb, LANES), _F32),
                        pltpu.VMEM((D_MODEL // LANES, rows, LANES), _F32)],
        compiler_params=pltpu.CompilerParams(dimension_semantics=("arbitrary",), vmem_limit_bytes=VMEM_LIMIT),
        name="s5_layer",
    )(x, ada_all, h0, *weights)


def _rope_angles(pos):
    half = HEAD_DIM // 2
    inv_freq = np.power(ROPE_THETA, -np.arange(half, dtype=np.float64) / half)
    ang = pos.astype(np.float64)[:, None] * inv_freq[None, :]
    return np.cos(ang).astype(np.float32), np.sin(ang).astype(np.float32)


def _rope_tables(pos):
    cos, sin = _rope_angles(pos)
    zero = np.zeros_like(sin)
    cos_t = np.tile(cos, (1, LANES // (HEAD_DIM // 2)))
    sin_lo = np.tile(np.concatenate([-sin, zero], axis=1), (1, LANES // HEAD_DIM))
    sin_hi = np.tile(np.concatenate([zero, sin], axis=1), (1, LANES // HEAD_DIM))
    return jnp.asarray(cos_t), jnp.asarray(sin_lo), jnp.asarray(sin_hi)


def _rope_tile(x, cos_t, sin_lo, sin_hi):
    half = HEAD_DIM // 2
    return x * cos_t + pltpu.roll(x, LANES - half, 1) * sin_lo + pltpu.roll(x, half, 1) * sin_hi


def _kv_kernel(x_ref, w_ref, cos_ref, slo_ref, shi_ref, k_ref, v_ref, kbf_ref):
    kv = _dot(x_ref[0].astype(_BF), w_ref[...])
    cos_t, sin_lo, sin_hi = cos_ref[...], slo_ref[...], shi_ref[...]
    for t in range(KV_WIDTH // LANES):
        tile = _rope_tile(kv[:, t * LANES:(t + 1) * LANES], cos_t, sin_lo, sin_hi)
        k_ref[0, :, t * LANES:(t + 1) * LANES] = tile
        kbf_ref[0, :, t * LANES:(t + 1) * LANES] = tile.astype(_BF)
    v_ref[0] = kv[:, KV_WIDTH:]


def _shared_kv(x, w_kv, tables):
    bsz, seq, _ = x.shape
    rows = min(KV_ROWS, seq)
    tab_spec = pl.BlockSpec((rows, LANES), lambda b, i: (i, 0))
    kv_spec = pl.BlockSpec((1, rows, KV_WIDTH), lambda b, i: (b, i, 0))
    out_specs = [kv_spec] * 3
    out_shape = ([jax.ShapeDtypeStruct((bsz, seq, KV_WIDTH), _F32)] * 2
                 + [jax.ShapeDtypeStruct((bsz, seq, KV_WIDTH), _BF)])
    return pl.pallas_call(
        _kv_kernel,
        grid=(bsz, seq // rows),
        in_specs=[pl.BlockSpec((1, rows, D_MODEL), lambda b, i: (b, i, 0)),
                  _const_spec((D_MODEL, 2 * KV_WIDTH)), tab_spec, tab_spec, tab_spec],
        out_specs=out_specs, out_shape=out_shape,
        compiler_params=pltpu.CompilerParams(dimension_semantics=("arbitrary", "arbitrary"),
                                             vmem_limit_bytes=VMEM_LIMIT),
        name="shared_kv",
    )(x, w_kv, *tables)


def _reduce_rows(x, pair_op, final_op):
    sub = 8
    parts = [x[i * sub:(i + 1) * sub] for i in range(x.shape[0] // sub)]
    while len(parts) > 1:
        parts = [pair_op(parts[i], parts[i + 1]) for i in range(0, len(parts), 2)]
    return final_op(parts[0], axis=0, keepdims=True)


def _attn_layer_kernel(sinks_ref, x_ref, ada_ref, win_ref, wout_ref, cos_ref, sin_ref,
                       lng_ref, lnb_ref, *refs, bb, rb, row0, layer_b, banded, make_kv):
    rows = bb * rb
    n_pairs = rows // LANES
    pairs_per_stream = rb // LANES
    if make_kv:
        (wkv_ref, cosr_ref, slo_ref, shi_ref, y_ref, kcache_ref, vcache_ref, kbf_ref, vtout_ref,
         hm_scr, blk_scr, qt_scr, gt_scr, ot_scr, ob_scr, acc_scr, cap_scr, k_src, vt_src) = refs
        step = pl.program_id(1)
        kv_new = _dot(x_ref[0].astype(_BF), wkv_ref[...])
        first_row = pl.multiple_of(step * rb, rb)
        for t in range(KV_WIDTH // LANES):
            tile = _rope_tile(kv_new[:, t * LANES:(t + 1) * LANES], cosr_ref[...], slo_ref[...], shi_ref[...])
            kcache_ref[0, :, t * LANES:(t + 1) * LANES] = tile[rb - WINDOW:]
            kbf_ref[0, :, t * LANES:(t + 1) * LANES] = tile.astype(_BF)
            k_src[0, pl.ds(first_row, rb), t * LANES:(t + 1) * LANES] = tile.astype(_BF)
        v_new = kv_new[:, KV_WIDTH:]
        vcache_ref[0] = v_new[rb - WINDOW:]
        for blk in range(rb // LANES):
            vt_blk = v_new[blk * LANES:(blk + 1) * LANES].T.astype(_BF)
            vtout_ref[0, blk] = vt_blk
            vt_src[0, step * (rb // LANES) + blk] = vt_blk
    else:
        (k_src, vt_src, y_ref, hm_scr, blk_scr, qt_scr, gt_scr, ot_scr, ob_scr, acc_scr, cap_scr) = refs

    def ada_row(k, b):
        if banded:
            return ada_ref[k, pl.ds(row0 + pl.program_id(0) * bb + b, 1), :]
        return ada_ref[k, row0 + b:row0 + b + 1, :]

    for b in range(bb):
        hm_scr[b * rb:(b + 1) * rb] = (x_ref[b] * (1.0 + ada_row(1, b)) + ada_row(0, b)).astype(_BF)
    cos_t, sin_t = cos_ref[...], sin_ref[...]
    half = HEAD_DIM // 2
    qk_scale = HEAD_DIM ** -0.5 * LOG2_E
    kt = IN_PIECE

    def in_piece(c, k):
        part = _dot(hm_scr[:, k * kt:(k + 1) * kt], win_ref[k * kt:(k + 1) * kt, c * HEAD_TILE:(c + 1) * HEAD_TILE])
        if k == 0:
            blk_scr[c // N_KV_HEADS] = part
        else:
            blk_scr[c // N_KV_HEADS] += part

    def in_finish(c):
        t = blk_scr[c // N_KV_HEADS].T
        if c < N_KV_HEADS:
            for j in range(Q_PER_KV):
                lo = c * HEAD_TILE + j * HEAD_DIM
                x1, x2 = t[j * HEAD_DIM:j * HEAD_DIM + half], t[j * HEAD_DIM + half:(j + 1) * HEAD_DIM]
                qt_scr[lo:lo + half] = ((x1 * cos_t - x2 * sin_t) * qk_scale).astype(_BF)
                qt_scr[lo + half:lo + HEAD_DIM] = ((x2 * cos_t + x1 * sin_t) * qk_scale).astype(_BF)
        else:
            lo = (c - N_KV_HEADS) * HEAD_TILE
            gt_scr[lo:lo + HEAD_TILE] = jax.nn.silu(t)

    def out_prepare(kv):
        ob_scr[...] = ot_scr[kv * HEAD_TILE:(kv + 1) * HEAD_TILE].T.astype(_BF)

    def out_piece(kv, n):
        part = _dot(ob_scr[...], wout_ref[kv * HEAD_TILE:(kv + 1) * HEAD_TILE, n * HEAD_TILE:(n + 1) * HEAD_TILE])
        if kv == 0:
            acc_scr[:, n * HEAD_TILE:(n + 1) * HEAD_TILE] = part
        else:
            acc_scr[:, n * HEAD_TILE:(n + 1) * HEAD_TILE] += part

    key_blk = lax.broadcasted_iota(jnp.int32, (PAIR_KEYS, LANES), 0) // CHUNK
    qry_blk = lax.broadcasted_iota(jnp.int32, (PAIR_KEYS, LANES), 1) // CHUNK
    no_q = jnp.zeros((HEAD_DIM, LANES), _BF)
    pairs = []
    for a in range(n_pairs):
        if banded:
            qc0 = pl.program_id(1) * (rb // CHUNK) + 2 * (a % pairs_per_stream)
            kc0 = jnp.maximum(qc0 - WINDOW_CHUNKS, 0)
            delta0 = kc0 - qc0
            kstart = pl.multiple_of(kc0 * CHUNK, LANES)
            vblk = kc0 // 2
        else:
            delta0, kstart, vblk = -WINDOW_CHUNKS, 0, 0
        rel = delta0 + key_blk - qry_blk
        cap_scr[a] = jnp.where((rel >= -WINDOW_CHUNKS) & (rel <= 0), MASK_PASS, NEG_INF)
        pairs.append((a // pairs_per_stream, kstart, vblk))

    def unit_heads(kv, gp):
        return (kv * Q_PER_KV + 2 * gp, kv * Q_PER_KV + 2 * gp + 1)

    def scores(a, kv, gp):
        lanes = slice(a * LANES, (a + 1) * LANES)
        stream, kstart, _ = pairs[a]
        k_tile = k_src[stream, pl.ds(kstart, PAIR_KEYS), (kv // 2) * LANES:(kv // 2 + 1) * LANES]
        w_parts = []
        for h in unit_heads(kv, gp):
            q_h = qt_scr[h * HEAD_DIM:(h + 1) * HEAD_DIM, lanes]
            w_parts.append(jnp.concatenate([q_h, no_q] if kv % 2 == 0 else [no_q, q_h], axis=0))
        return _dot(k_tile, jnp.concatenate(w_parts, axis=1))

    def softmax_values(a, kv, gp, s):
        e_parts, inv_den = [], []
        for u, h in enumerate(unit_heads(kv, gp)):
            s_h = jnp.minimum(s[:, u * LANES:(u + 1) * LANES], cap_scr[a])
            sink = sinks_ref[layer_b, h] * LOG2_E
            m = jnp.maximum(_reduce_rows(s_h, jnp.maximum, jnp.max), sink)
            e = jnp.exp2(s_h - m)
            inv_den.append(1.0 / (_reduce_rows(e, jnp.add, jnp.sum) + jnp.exp2(sink - m)))
            e_parts.append(e.astype(_BF))
        stream, _, vblk = pairs[a]
        vt_kv = jnp.concatenate([vt_src[stream, vblk, kv * HEAD_DIM:(kv + 1) * HEAD_DIM, :],
                                 vt_src[stream, vblk + 1, kv * HEAD_DIM:(kv + 1) * HEAD_DIM, :]], axis=1)
        return _dot(vt_kv, jnp.concatenate(e_parts, axis=1)), inv_den

    def finish(a, kv, gp, ot, inv_den):
        lanes = slice(a * LANES, (a + 1) * LANES)
        for u, h in enumerate(unit_heads(kv, gp)):
            rows_h = slice(h * HEAD_DIM, (h + 1) * HEAD_DIM)
            ot_scr[rows_h, lanes] = ot[:, u * LANES:(u + 1) * LANES] * inv_den[u] * gt_scr[rows_h, lanes]

    n_k = D_MODEL // kt

    def in_block(c):
        return [functools.partial(in_piece, c, k) for k in range(n_k)] + [functools.partial(in_finish, c)]

    for item in in_block(0) + in_block(N_KV_HEADS):
        item()
    slots = []
    for kv in range(N_KV_HEADS):
        units = [(a, kv, gp) for a in range(n_pairs) for gp in range(Q_PER_KV // 2)]
        work = [[] for _ in units]
        ahead, behind = [], []
        if kv + 1 < N_KV_HEADS:
            ahead = in_block(kv + 1) + in_block(kv + 1 + N_KV_HEADS)
        if kv > 0:
            behind = [functools.partial(out_prepare, kv - 1)] + [
                functools.partial(out_piece, kv - 1, n) for n in range(D_MODEL // HEAD_TILE)]
        for m, item in enumerate(ahead):
            work[m * len(units) // len(ahead)].append(item)
        for m, item in enumerate(behind):
            work[1 + m * (len(units) - 1) // len(behind)].append(item)
        slots += list(zip(units, work))
    pending = [scores(*slots[k][0]) for k in range(SCORE_AHEAD)]
    unfinished = None
    for i, (unit, work) in enumerate(slots):
        if i + SCORE_AHEAD < len(slots):
            pending.append(scores(*slots[i + SCORE_AHEAD][0]))
        s_cur = pending.pop(0)
        for item in work:
            item()
        ot, inv_den = softmax_values(*unit, s_cur)
        if unfinished is not None:
            finish(*unfinished)
        unfinished = (*unit, ot, inv_den)
    finish(*unfinished)
    out_prepare(N_KV_HEADS - 1)
    half_rows = rows // 2
    seg = min(rb, half_rows)
    w_last = wout_ref[(N_KV_HEADS - 1) * HEAD_TILE:N_KV_HEADS * HEAD_TILE, :]
    outs = [acc_scr[k * half_rows:(k + 1) * half_rows] + _dot(ob_scr[k * half_rows:(k + 1) * half_rows], w_last)
            for k in range(2)]
    for k in range(2):
        for s in range(half_rows // seg):
            r0 = k * half_rows + s * seg
            b, q0 = r0 // rb, r0 % rb
            r = DEEPNORM_ALPHA * x_ref[b, q0:q0 + seg] + ada_row(2, b) * outs[k][s * seg:(s + 1) * seg]
            y_ref[b, q0:q0 + seg] = _layer_norm(r, lng_ref[...], lnb_ref[...])


def _attn_layer(x, ada_all, row0, lb, p, tables_t, banded, bb, rb, kv=None, kv_maker=None):
    bsz, seq, _ = x.shape
    rows = bb * rb
    layer = N_A_LAYERS + lb
    row_spec = pl.BlockSpec((bb, rb, D_MODEL), lambda b, i: (b, i, 0))
    tab_spec = pl.BlockSpec((HEAD_DIM // 2, rows), lambda b, i: (0, i))
    y_shape = jax.ShapeDtypeStruct((bsz, seq, D_MODEL), _F32)
    kern = functools.partial(_attn_layer_kernel, bb=bb, rb=rb, row0=row0, layer_b=lb, banded=banded,
                             make_kv=kv is None)
    if kv is None:
        assert bb == 1
        w_kv, row_tables = kv_maker
        rtab_spec = pl.BlockSpec((rb, LANES), lambda b, i: (i, 0))
        kv_in, kv_specs = (w_kv, *row_tables), [_const_spec((D_MODEL, 2 * KV_WIDTH))] + [rtab_spec] * 3
        cache_spec = pl.BlockSpec((1, WINDOW, KV_WIDTH), lambda b, i: (b, 0, 0))
        out_specs = [row_spec, cache_spec, cache_spec,
                     pl.BlockSpec((1, rb, KV_WIDTH), lambda b, i: (b, i, 0)),
                     pl.BlockSpec((1, rb // LANES, KV_WIDTH, LANES), lambda b, i: (b, i, 0, 0))]
        out_shape = [y_shape] + [jax.ShapeDtypeStruct((bsz, WINDOW, KV_WIDTH), _F32)] * 2 + [
            jax.ShapeDtypeStruct((bsz, seq, KV_WIDTH), _BF),
            jax.ShapeDtypeStruct((bsz, seq // LANES, KV_WIDTH, LANES), _BF)]
        kv_scratch = [pltpu.VMEM((1, seq, KV_WIDTH), _BF), pltpu.VMEM((1, seq // LANES, KV_WIDTH, LANES), _BF)]
    else:
        keys = kv[0].shape[1]
        kv_in, kv_specs = kv, [pl.BlockSpec((bb, keys, KV_WIDTH), lambda b, i: (b, 0, 0)),
                               pl.BlockSpec((bb, keys // LANES, KV_WIDTH, LANES), lambda b, i: (b, 0, 0, 0))]
        out_specs, out_shape, kv_scratch = row_spec, y_shape, []
    return pl.pallas_call(
        kern,
        grid=(bsz // bb, seq // rb),
        in_specs=[pl.BlockSpec(memory_space=pltpu.SMEM),
                  row_spec, _layer_spec(ada_all.shape, layer),
                  _layer_spec(p['win_b'].shape, lb), _layer_spec(p['w_out_b'].shape, lb),
                  tab_spec, tab_spec,
                  _layer_spec(p['ln_g'].shape, layer), _layer_spec(p['ln_b'].shape, layer)] + kv_specs,
        out_specs=out_specs,
        out_shape=out_shape,
        scratch_shapes=[pltpu.VMEM((rows, D_MODEL), _BF),
                        pltpu.VMEM((2, rows, HEAD_TILE), _F32),
                        pltpu.VMEM((D_MODEL, rows), _BF),
                        pltpu.VMEM((D_MODEL, rows), _F32),
                        pltpu.VMEM((D_MODEL, rows), _F32),
                        pltpu.VMEM((rows, HEAD_TILE), _BF),
                        pltpu.VMEM((rows, D_MODEL), _F32),
                        pltpu.VMEM((rows // LANES, PAIR_KEYS, LANES), _F32)]
        + kv_scratch,
        compiler_params=pltpu.CompilerParams(dimension_semantics=("arbitrary", "arbitrary"),
                                             vmem_limit_bytes=VMEM_LIMIT),
        name="attn_layer",
    )(p['attn_sinks'], x, ada_all, p['win_b'], p['w_out_b'], *tables_t, p['ln_g'], p['ln_b'], *kv_in)


def _run_trunk(x, ada_all, row0, pos, h0, cache_k, cache_v, p):
    bsz, seq, _ = x.shape
    states = []
    for la in range(N_A_LAYERS):
        x, h_last = _s5_layer(x, bsz, seq, ada_all, row0, h0, la, p,
                              batch_major_in=la == 0, batch_major_out=la == N_A_LAYERS - 1)
        states.append(_strips_to_state(h_last))
    banded = cache_k is None
    row_tables = _rope_tables(pos)
    if banded:
        kv, kv_maker, q_pos = None, (p['w_kv'], row_tables), pos
        bb, rb = 1, ATTN_ROWS
    else:
        k_new, v_new, k_bf = _shared_kv(x, p['w_kv'], row_tables)
        pad = LANES - seq
        flat = lambda c: c.reshape(bsz, c.shape[1], KV_WIDTH)
        k_bf = jnp.concatenate([flat(cache_k).astype(_BF), k_bf, jnp.zeros((bsz, pad, KV_WIDTH), _BF)], axis=1)
        v_all = jnp.concatenate([flat(cache_v), v_new, jnp.zeros((bsz, pad, KV_WIDTH), _F32)], axis=1)
        vt = v_all.reshape(bsz, PAIR_KEYS // LANES, LANES, KV_WIDTH).transpose(0, 1, 3, 2).astype(_BF)
        kv, kv_maker = (k_bf, vt), None
        x = jnp.pad(x, ((0, 0), (0, pad), (0, 0)))
        q_pos = np.tile(pos[0] + np.arange(LANES), bsz)
        bb, rb = bsz, LANES
    cos, sin = _rope_angles(q_pos)
    tables_t = (jnp.asarray(cos.T), jnp.asarray(sin.T))
    for lb in range(N_B_LAYERS):
        out = _attn_layer(x, ada_all, row0, lb, p, tables_t, banded=banded, bb=bb, rb=rb, kv=kv, kv_maker=kv_maker)
        if kv is None:
            x, k_new, v_new, k_bf, vt = out
            kv = (k_bf, vt)
        else:
            x = out
    k4 = k_new.reshape(bsz, k_new.shape[1], N_KV_HEADS, HEAD_DIM)
    v4 = v_new.reshape(bsz, v_new.shape[1], N_KV_HEADS, HEAD_DIM)
    return x[:, :seq], jnp.stack(states), k4, v4


def kernel(x_prompt, x_sample, state_ssm, cache_k, cache_v, c_prompt, c_sample, w_ada, b_ada, ln_g, ln_b, w_in_a,
           ssm_a_re, ssm_a_im, ssm_b_re, ssm_b_im, ssm_c_re, ssm_c_im, ssm_d, ssm_log_dt, w_glu, b_glu, w_out_a,
           w_kv, w_in_b, attn_sinks, w_out_b):
    n_prompt, n_sample = x_prompt.shape[0], x_sample.shape[0]
    ada_all = _ada_params(jnp.concatenate([c_prompt, c_sample], axis=0), w_ada, b_ada)
    bs, cs, ks, ar, ai = _s5_matrices(
        _s5_prep(ssm_a_re, ssm_a_im, ssm_log_dt, ssm_b_re, ssm_b_im, ssm_c_re, ssm_c_im))
    p = dict(win_a=w_in_a.astype(_BF), bs=bs, cs=cs, ks=ks, ar=ar, ai=ai,
             ssm_d=ssm_d.reshape(N_A_LAYERS, N_STRIPS, 1, LANES), w_glu=w_glu.astype(_BF),
             b_glu=b_glu.reshape(N_A_LAYERS, 1, D_MODEL), w_out_a=w_out_a.astype(_BF),
             ln_g=ln_g.reshape(DEPTH, 1, D_MODEL), ln_b=ln_b.reshape(DEPTH, 1, D_MODEL), w_kv=w_kv.astype(_BF),
             win_b=w_in_b.astype(_BF), w_out_b=w_out_b.astype(_BF), attn_sinks=attn_sinks)

    pos_prompt = np.arange(x_prompt.shape[1])
    pos_sample = PAST_LEN + np.arange(x_sample.shape[1])
    h0_prompt = jnp.zeros((N_A_LAYERS, N_STRIPS, n_prompt, STRIP_W), _F32)
    h0_sample = jnp.stack([_state_to_strips(state_ssm[la]) for la in range(N_A_LAYERS)])

    y_p, ssm_p, k_p, v_p = _run_trunk(x_prompt, ada_all, 0, pos_prompt, h0_prompt, None, None, p)
    y_s, ssm_s, k_s, v_s = _run_trunk(x_sample, ada_all, n_prompt, pos_sample, h0_sample, cache_k, cache_v, p)
    rows = min(WINDOW, x_prompt.shape[1])
    return (y_p, y_s, ssm_p, k_p[:, -rows:], v_p[:, -rows:], ssm_s, k_s, v_s)
```
